```python
import math
import jax
import jax.numpy as jnp
from jax import lax
import numpy as np

D_MODEL = 2048
BATCH = 8
SEQ = 2048
DEPTH = 2

N_A_LAYERS = DEPTH // 2
N_B_LAYERS = DEPTH - N_A_LAYERS

NH_A = 8
DK_A = D_MODEL // 2 // NH_A
DV_A = D_MODEL // NH_A
QK_A = NH_A * DK_A
V_A = NH_A * DV_A
CONV_A = 4
CHUNK = 64
A_IN_COLS = 2 * QK_A + 3 * V_A + 2 * NH_A

NH_B = 16
DH_B = 128
W_B = NH_B * DH_B
QBLOCK = 128

ALPHA = (2.0 * DEPTH) ** 0.25
BETA = (8.0 * DEPTH) ** -0.25
LN_EPS = 1e-5

kernel_name = "yoco_mlstm_stickbreaking_hybrid"


def _layernorm(x, g, b):
    xf = x.astype(jnp.float32)
    mu = jnp.mean(xf, axis=-1, keepdims=True)
    xc = xf - mu
    var = jnp.mean(xc * xc, axis=-1, keepdims=True)
    return (xc * lax.rsqrt(var + LN_EPS) * g.astype(jnp.float32) + b.astype(jnp.float32)).astype(x.dtype)


def _causal_depthwise_conv(u, w, b):
    c = u.shape[-1]
    out = lax.conv_general_dilated(
        u, w[:, None, :], window_strides=(1,), padding=[(CONV_A - 1, 0)],
        dimension_numbers=("NWC", "WIO", "NWC"), feature_group_count=c)
    return out + b


def _mlstm_chunkwise(q, k, v, ig, lf):
    bsz, nh, s, dk = q.shape
    dv = v.shape[-1]
    nc = s // CHUNK

    def to_chunks(a):
        return jnp.moveaxis(a.reshape(bsz, nh, nc, CHUNK, *a.shape[3:]), 2, 0)

    causal = jnp.tril(jnp.ones((CHUNK, CHUNK), dtype=bool))

    def step(carry, xs):
        c_st, n_st, m_st = carry
        qc, kc, vc, ic, fc = xs
        bcum = jnp.cumsum(fc, axis=-1)
        gtot = bcum[..., -1]
        dmat = jnp.where(causal, bcum[..., :, None] - bcum[..., None, :] + ic[..., None, :], -jnp.inf)
        inter = bcum + m_st[..., None]
        m_q = jnp.maximum(inter, jnp.max(dmat, axis=-1))
        scores = jnp.einsum("bhjd,bhsd->bhjs", qc, kc) * jnp.exp(dmat - m_q[..., None])
        w_inter = jnp.exp(inter - m_q)
        num = jnp.einsum("bhjs,bhsv->bhjv", scores, vc) + w_inter[..., None] * jnp.einsum("bhjd,bhdv->bhjv", qc, c_st)
        den = jnp.sum(scores, axis=-1) + w_inter * jnp.einsum("bhjd,bhd->bhj", qc, n_st)
        h = num / jnp.maximum(jnp.abs(den), jnp.exp(-m_q))[..., None]
        wlog = gtot[..., None] - bcum + ic
        m_new = jnp.maximum(gtot + m_st, jnp.max(wlog, axis=-1))
        wk = jnp.exp(wlog - m_new[..., None])
        decay = jnp.exp(gtot + m_st - m_new)
        c_new = decay[..., None, None] * c_st + jnp.einsum("bhsd,bhsv->bhdv", kc * wk[..., None], vc)
        n_new = decay[..., None] * n_st + jnp.einsum("bhs,bhsd->bhd", wk, kc)
        return (c_new, n_new, m_new), h

    init = (jnp.zeros((bsz, nh, dk, dv), jnp.float32),
            jnp.zeros((bsz, nh, dk), jnp.float32),
            jnp.zeros((bsz, nh), jnp.float32))
    _, hs = lax.scan(step, init, (to_chunks(q), to_chunks(k), to_chunks(v), to_chunks(ig), to_chunks(lf)))
    return jnp.moveaxis(hs, 0, 2).reshape(bsz, nh, s, dv)


def _mlstm_layer(x, w_in, gate_b, conv_w, conv_b, head_g, w_out, ln_g, ln_b):
    bsz, s, _ = x.shape
    u = jnp.einsum("bsd,de->bse", x, w_in)
    qk, v, o, z, gates = jnp.split(u, [2 * QK_A, 2 * QK_A + V_A, 2 * QK_A + 2 * V_A, 2 * QK_A + 3 * V_A], axis=-1)
    qk = jax.nn.silu(_causal_depthwise_conv(qk, conv_w, conv_b))
    q, k = jnp.split(qk, 2, axis=-1)
    gates = (gates + gate_b).astype(jnp.float32)
    ig, fg = jnp.split(gates, 2, axis=-1)

    def heads(a, d):
        return a.reshape(bsz, s, NH_A, d).transpose(0, 2, 1, 3).astype(jnp.float32)

    qh = heads(q, DK_A)
    kh = heads(k, DK_A) * (DK_A ** -0.5)
    vh = heads(v, DV_A)
    ig_h = ig.transpose(0, 2, 1)
    lf_h = jax.nn.log_sigmoid(fg).transpose(0, 2, 1)
    h = _mlstm_chunkwise(qh, kh, vh, ig_h, lf_h)
    mu = jnp.mean(h, axis=-1, keepdims=True)
    hc = h - mu
    h = hc * lax.rsqrt(jnp.mean(hc * hc, axis=-1, keepdims=True) + LN_EPS)
    h = h.transpose(0, 2, 1, 3).reshape(bsz, s, V_A).astype(x.dtype) * head_g
    h = jax.nn.sigmoid(o) * h * jax.nn.silu(z)
    y = jnp.einsum("bse,ed->bsd", h, w_out)
    return _layernorm(ALPHA * x + y, ln_g, ln_b)


def _stick_breaking(q, k, v):
    s = q.shape[2]
    scale = DH_B ** -0.5
    outs = []
    for blk in range(s // QBLOCK):
        t0 = blk * QBLOCK
        t1 = t0 + QBLOCK
        qb = q[:, :, t0:t1].astype(jnp.float32)
        kp = k[:, :, :t1].astype(jnp.float32)
        vp = v[:, :, :t1].astype(jnp.float32)
        z = jnp.einsum("bhtd,bhsd->bhts", qb, kp) * scale
        mask = jnp.arange(t1)[None, :] < (t0 + jnp.arange(QBLOCK))[:, None]
        log_1mb = jnp.where(mask, jax.nn.log_sigmoid(-z), 0.0)
        between = lax.cumsum(log_1mb, axis=3, reverse=True) - log_1mb
        a = jnp.where(mask, jnp.exp(jax.nn.log_sigmoid(z) + between), 0.0)
        outs.append(jnp.einsum("bhts,bhsd->bhtd", a, vp))
    return jnp.concatenate(outs, axis=2)


def _stick_breaking_layer(x, k_sh, v_sh, w_in, w_out, ln_g, ln_b):
    bsz, s, _ = x.shape
    u = jnp.einsum("bsd,de->bse", x, w_in)
    q, z = jnp.split(u, [W_B], axis=-1)
    qh = q.reshape(bsz, s, NH_B, DH_B).transpose(0, 2, 1, 3)
    att = _stick_breaking(qh, k_sh, v_sh)
    att = att.transpose(0, 2, 1, 3).reshape(bsz, s, W_B).astype(x.dtype)
    y = jnp.einsum("bse,ed->bsd", att * jax.nn.silu(z), w_out)
    return _layernorm(ALPHA * x + y, ln_g, ln_b)


def setup_inputs(seed: int = 0) -> dict:
    key = jax.random.key(seed)
    ks = jax.random.split(key, 16)
    f32 = jnp.float32
    nrm = lambda k, shape: jax.random.normal(k, shape, f32)
    x = nrm(ks[0], (BATCH, SEQ, D_MODEL))
    a_w_in = nrm(ks[1], (N_A_LAYERS, D_MODEL, A_IN_COLS)) * D_MODEL ** -0.5
    i_bias = 0.1 * nrm(ks[2], (N_A_LAYERS, NH_A))
    f_bias = jnp.linspace(3.0, 6.0, NH_A, dtype=f32)[None, :] + 0.1 * nrm(ks[3], (N_A_LAYERS, NH_A))
    a_gate_b = jnp.concatenate([i_bias, f_bias], axis=-1)
    a_conv_w = nrm(ks[4], (N_A_LAYERS, CONV_A, 2 * QK_A)) * CONV_A ** -0.5
    a_conv_b = 0.01 * nrm(ks[5], (N_A_LAYERS, 2 * QK_A))
    a_head_g = 1.0 + 0.02 * nrm(ks[6], (N_A_LAYERS, V_A))
    a_w_out = nrm(ks[7], (N_A_LAYERS, V_A, D_MODEL)) * (V_A ** -0.5) * BETA
    a_ln_g = 1.0 + 0.02 * nrm(ks[8], (N_A_LAYERS, D_MODEL))
    a_ln_b = 0.02 * nrm(ks[9], (N_A_LAYERS, D_MODEL))
    kv_w = nrm(ks[10], (D_MODEL, 2 * W_B)) * D_MODEL ** -0.5
    b_w_in = nrm(ks[11], (N_B_LAYERS, D_MODEL, 2 * W_B)) * D_MODEL ** -0.5
    b_w_out = nrm(ks[12], (N_B_LAYERS, W_B, D_MODEL)) * (W_B ** -0.5) * BETA
    b_ln_g = 1.0 + 0.02 * nrm(ks[13], (N_B_LAYERS, D_MODEL))
    b_ln_b = 0.02 * nrm(ks[14], (N_B_LAYERS, D_MODEL))
    return {"x": x, "a_w_in": a_w_in, "a_gate_b": a_gate_b, "a_conv_w": a_conv_w, "a_conv_b": a_conv_b,
            "a_head_g": a_head_g, "a_w_out": a_w_out, "a_ln_g": a_ln_g, "a_ln_b": a_ln_b,
            "kv_w": kv_w, "b_w_in": b_w_in, "b_w_out": b_w_out, "b_ln_g": b_ln_g, "b_ln_b": b_ln_b}


def reference(x, a_w_in, a_gate_b, a_conv_w, a_conv_b, a_head_g, a_w_out, a_ln_g, a_ln_b,
              kv_w, b_w_in, b_w_out, b_ln_g, b_ln_b):
    bsz, s, _ = x.shape
    k_sh = None
    v_sh = None
    for layer in range(DEPTH):
        if layer < N_A_LAYERS:
            x = _mlstm_layer(x, a_w_in[layer], a_gate_b[layer], a_conv_w[layer], a_conv_b[layer],
                             a_head_g[layer], a_w_out[layer], a_ln_g[layer], a_ln_b[layer])
        else:
            if layer == N_A_LAYERS:
                kv = jnp.einsum("bsd,de->bse", x, kv_w)
                k_sh, v_sh = jnp.split(kv, 2, axis=-1)
                k_sh = k_sh.reshape(bsz, s, NH_B, DH_B).transpose(0, 2, 1, 3)
                v_sh = v_sh.reshape(bsz, s, NH_B, DH_B).transpose(0, 2, 1, 3)
            j = layer - N_A_LAYERS
            x = _stick_breaking_layer(x, k_sh, v_sh, b_w_in[j], b_w_out[j], b_ln_g[j], b_ln_b[j])
    return x
```

```python
import functools

import jax
import jax.numpy as jnp
from jax import lax
from jax.experimental import pallas as pl
from jax.experimental.pallas import tpu as pltpu

F32 = jnp.float32
BF16 = jnp.bfloat16

D_MODEL = 2048
NH_A = 8
DK_A = 128
DV_A = 256
QK_A = NH_A * DK_A
V_A = NH_A * DV_A
CONV_A = 4
NH_B = 16
DH_B = 128
W_B = NH_B * DH_B
DEPTH = 2
ALPHA = (2.0 * DEPTH) ** 0.25
LN_EPS = 1e-5

LANES = 128
VMEM_LIMIT = 48 * 1024 * 1024

MLSTM_CHUNK = 128
CONV_TILE = 256
SB_TILE = 256
NEG_BIG = -1e30


def _dot(a, b):
    return jnp.dot(a, b, preferred_element_type=F32)


def _dot_nt(a, b):
    return lax.dot_general(a, b, (((1,), (1,)), ((), ())), preferred_element_type=F32)


def _dot_tn(a, b):
    return lax.dot_general(a, b, (((0,), (0,)), ((), ())), preferred_element_type=F32)


def _split_bf16(x):
    hi = x.astype(BF16)
    lo = (x - hi.astype(F32)).astype(BF16)
    return hi, lo


def _softplus(z):
    return jnp.maximum(z, 0.0) + jnp.log1p(jnp.exp(-jnp.abs(z)))


def _a_in_kernel(x_ref, wqk_ref, wv_ref, wo_ref, wz_ref, wg_ref, gb_ref,
                 qk_ref, v_ref, gate_ref, gates_ref, xb_ref):
    @pl.when(pl.program_id(1) == 0)
    def _():
        xb = x_ref[...].astype(BF16)
        xb_ref[...] = xb
        g = _dot(xb, wg_ref[...]) + gb_ref[...]
        lane = lax.broadcasted_iota(jnp.int32, g.shape, 1)
        log_f = jnp.minimum(g, 0.0) - jnp.log1p(jnp.exp(-jnp.abs(g)))
        gates_ref[...] = jnp.where(lane < NH_A, g, log_f)

    xb = xb_ref[...]
    qk_ref[...] = _dot(xb, wqk_ref[...]).astype(BF16)
    v_ref[...] = _dot(xb, wv_ref[...]).astype(BF16)
    o = _dot(xb, wo_ref[...])
    z = _dot(xb, wz_ref[...])
    gate_ref[...] = (jax.nn.sigmoid(o) * z * jax.nn.sigmoid(z)).astype(BF16)


def _a_in_proj(x2, w_main, w_g, gate_b, tm=1024, tn=256):
    n = x2.shape[0]
    nb = D_MODEL // tn
    wspec = lambda g: pl.BlockSpec((D_MODEL, tn), lambda i, j, g=g: (0, g * nb + j))
    ospec = pl.BlockSpec((tm, tn), lambda i, j: (i, j))
    return pl.pallas_call(
        _a_in_kernel,
        grid=(n // tm, nb),
        in_specs=[pl.BlockSpec((tm, D_MODEL), lambda i, j: (i, 0)),
                  wspec(0), wspec(1), wspec(2), wspec(3),
                  pl.BlockSpec((D_MODEL, 2 * NH_A), lambda i, j: (0, 0)),
                  pl.BlockSpec((1, 2 * NH_A), lambda i, j: (0, 0))],
        out_specs=[ospec, ospec, ospec, pl.BlockSpec((tm, 2 * NH_A), lambda i, j: (i, 0))],
        out_shape=[jax.ShapeDtypeStruct((n, 2 * QK_A), BF16),
                   jax.ShapeDtypeStruct((n, V_A), BF16),
                   jax.ShapeDtypeStruct((n, V_A), BF16),
                   jax.ShapeDtypeStruct((n, 2 * NH_A), F32)],
        scratch_shapes=[pltpu.VMEM((tm, D_MODEL), BF16)],
        compiler_params=pltpu.CompilerParams(
            dimension_semantics=("arbitrary", "arbitrary"), vmem_limit_bytes=VMEM_LIMIT),
        name="a_in_proj",
    )(x2, w_main, w_main, w_main, w_main, w_g, gate_b)


def _mlstm_kernel(qp_ref, kp_ref, v_ref, gate_ref, gates_ref, cwq_ref, cbq_ref, cwk_ref, cbk_ref, hg_ref,
                  out_ref, xf_ref, q_s, k_s, st_ref, m_ref):
    seq = qp_ref.shape[0]
    L = MLSTM_CHUNK
    head = pl.program_id(1)

    pad = 8
    for src, cw, cb, dst, scale in ((qp_ref, cwq_ref, cbq_ref, q_s, 1.0),
                                    (kp_ref, cwk_ref, cbk_ref, k_s, DK_A ** -0.5)):
        xf_ref[0:pad, :] = jnp.zeros((pad, LANES), F32)
        xf_ref[pad:, :] = src[...].astype(F32)
        w = cw[...]
        bias = cb[...]
        for t in range(seq // CONV_TILE):
            acc = jnp.broadcast_to(bias, (CONV_TILE, LANES))
            for kk in range(CONV_A):
                off = pad - (CONV_A - 1) + kk + t * CONV_TILE
                acc = acc + w[kk:kk + 1, :] * xf_ref[off:off + CONV_TILE, :]
            y = acc * jax.nn.sigmoid(acc)
            if scale != 1.0:
                y = y * scale
            dst[t * CONV_TILE:(t + 1) * CONV_TILE, :] = y.astype(BF16)

    st_ref[...] = jnp.zeros(st_ref.shape, F32)
    m_ref[...] = jnp.zeros(m_ref.shape, F32)

    ng = 2 * NH_A
    sel_r = lax.broadcasted_iota(jnp.int32, (ng, LANES), 0)
    oh_i = jnp.where(sel_r == head, 1.0, 0.0).astype(BF16)
    oh_f = jnp.where(sel_r == head + NH_A, 1.0, 0.0).astype(BF16)
    rr = lax.broadcasted_iota(jnp.int32, (L, L), 0)
    cc = lax.broadcasted_iota(jnp.int32, (L, L), 1)
    causal = cc <= rr
    tri = jnp.where(causal, 1.0, 0.0).astype(BF16)
    ones_v = jnp.ones((L, LANES), BF16)
    head_g = hg_ref[...]

    def chunk(c, carry):
        r0 = pl.multiple_of(c * L, L)
        q = q_s[pl.ds(r0, L), :]
        k = k_s[pl.ds(r0, L), :]
        vext = jnp.concatenate([v_ref[pl.ds(r0, L), :], ones_v], axis=1)
        g = gates_ref[pl.ds(r0, L), :]
        g_hi, g_lo = _split_bf16(g)
        ig = _dot(g_hi, oh_i) + _dot(g_lo, oh_i)
        lf = _dot(g_hi, oh_f) + _dot(g_lo, oh_f)
        lf_hi, lf_lo = _split_bf16(lf)
        bcum = _dot(tri, lf_hi) + _dot(tri, lf_lo)
        gtot = bcum[L - 1:L, :]
        m_st = m_ref[...]

        a_row = jnp.transpose(ig - bcum)
        dmat = jnp.where(causal, bcum + a_row, NEG_BIG)
        inter = bcum + m_st
        m_q = jnp.maximum(inter, jnp.max(dmat, axis=1, keepdims=True))
        p = jnp.exp(dmat - m_q)
        scores = (_dot_nt(q, k) * p).astype(BF16)
        intra = _dot(scores, vext)
        st = st_ref[...]
        carried = _dot(q, st.astype(BF16))
        w_inter = jnp.exp(inter - m_q)
        tot = intra + jnp.concatenate([w_inter] * 3, axis=1) * carried
        num = tot[:, :DV_A]
        den = tot[:, DV_A:]
        inv = 1.0 / jnp.maximum(jnp.abs(den), jnp.exp(-m_q))
        hh = num * jnp.concatenate([inv, inv], axis=1)

        mu = jnp.mean(hh, axis=1, keepdims=True)
        hc = hh - mu
        var = jnp.mean(hc * hc, axis=1, keepdims=True)
        hn = hc * lax.rsqrt(var + LN_EPS)
        gate = gate_ref[pl.ds(r0, L), :].astype(F32)
        out_ref[pl.ds(r0, L), :] = (hn * head_g * gate).astype(BF16)

        wlog = gtot - bcum + ig
        m_new = jnp.maximum(gtot + m_st, jnp.max(wlog, axis=0, keepdims=True))
        wk = jnp.exp(wlog - m_new)
        kw = (k.astype(F32) * wk).astype(BF16)
        decay = jnp.exp(gtot + m_st - m_new)
        st_ref[...] = jnp.concatenate([decay] * 3, axis=1) * st + _dot_tn(kw, vext)
        m_ref[...] = m_new
        return carry

    lax.fori_loop(0, seq // L, chunk, 0)


def _mlstm(qk, v, gate, gates, conv_w, conv_b, head_g, bsz, seq):
    n = bsz * seq
    return pl.pallas_call(
        _mlstm_kernel,
        grid=(bsz, NH_A),
        in_specs=[pl.BlockSpec((seq, DK_A), lambda b, h: (b, h)),
                  pl.BlockSpec((seq, DK_A), lambda b, h: (b, NH_A + h)),
                  pl.BlockSpec((seq, DV_A), lambda b, h: (b, h)),
                  pl.BlockSpec((seq, DV_A), lambda b, h: (b, h)),
                  pl.BlockSpec((seq, 2 * NH_A), lambda b, h: (b, 0)),
                  pl.BlockSpec((CONV_A, DK_A), lambda b, h: (0, h)),
                  pl.BlockSpec((1, DK_A), lambda b, h: (0, h)),
                  pl.BlockSpec((CONV_A, DK_A), lambda b, h: (0, NH_A + h)),
                  pl.BlockSpec((1, DK_A), lambda b, h: (0, NH_A + h)),
                  pl.BlockSpec((1, DV_A), lambda b, h: (0, h))],
        out_specs=pl.BlockSpec((seq, DV_A), lambda b, h: (b, h)),
        out_shape=jax.ShapeDtypeStruct((n, V_A), BF16),
        scratch_shapes=[pltpu.VMEM((seq + 8, LANES), F32),
                        pltpu.VMEM((seq, DK_A), BF16),
                        pltpu.VMEM((seq, DK_A), BF16),
                        pltpu.VMEM((DK_A, DV_A + LANES), F32),
                        pltpu.VMEM((1, LANES), F32)],
        compiler_params=pltpu.CompilerParams(
            dimension_semantics=("arbitrary", "arbitrary"), vmem_limit_bytes=VMEM_LIMIT),
        name="mlstm",
    )(qk, qk, v, gate, gates, conv_w, conv_b, conv_w, conv_b, head_g)


def _out_ln_kernel(h_ref, x_ref, w_ref, g_ref, b_ref, *out_refs):
    y = _dot(h_ref[...], w_ref[...])
    r = ALPHA * x_ref[...] + y
    mu = jnp.mean(r, axis=1, keepdims=True)
    rc = r - mu
    var = jnp.mean(rc * rc, axis=1, keepdims=True)
    out = rc * lax.rsqrt(var + LN_EPS) * g_ref[...] + b_ref[...]
    out_refs[0][...] = out
    if len(out_refs) > 1:
        out_refs[1][...] = out.astype(BF16)


def _out_ln(h, x2, w, ln_g, ln_b, with_bf16, tm=256):
    n = x2.shape[0]
    row = pl.BlockSpec((tm, D_MODEL), lambda i: (i, 0))
    vec = pl.BlockSpec((1, D_MODEL), lambda i: (0, 0))
    out_shape = [jax.ShapeDtypeStruct((n, D_MODEL), F32)]
    out_specs = [row]
    if with_bf16:
        out_shape.append(jax.ShapeDtypeStruct((n, D_MODEL), BF16))
        out_specs.append(row)
    return pl.pallas_call(
        _out_ln_kernel,
        grid=(n // tm,),
        in_specs=[row, row, pl.BlockSpec((D_MODEL, D_MODEL), lambda i: (0, 0)), vec, vec],
        out_specs=out_specs,
        out_shape=out_shape,
        compiler_params=pltpu.CompilerParams(
            dimension_semantics=("arbitrary",), vmem_limit_bytes=VMEM_LIMIT),
        name="out_ln",
    )(h, x2, w, ln_g, ln_b)


def _b_proj_kernel(x_ref, wk_ref, wv_ref, wq_ref, wz_ref, k_ref, v_ref, q_ref, zg_ref):
    xb = x_ref[...]
    k_ref[...] = _dot(xb, wk_ref[...]).astype(BF16)
    v_ref[...] = _dot(xb, wv_ref[...]).astype(BF16)
    q_ref[...] = (_dot(xb, wq_ref[...]) * (DH_B ** -0.5)).astype(BF16)
    z = _dot(xb, wz_ref[...])
    zg_ref[...] = (z * jax.nn.sigmoid(z)).astype(BF16)


def _b_proj(x1b, kv_w, b_w_in, tm=1024, tn=256):
    n = x1b.shape[0]
    nb = W_B // tn
    wspec = lambda g: pl.BlockSpec((D_MODEL, tn), lambda i, j, g=g: (0, g * nb + j))
    ospec = pl.BlockSpec((tm, tn), lambda i, j: (i, j))
    sds = jax.ShapeDtypeStruct((n, W_B), BF16)
    return pl.pallas_call(
        _b_proj_kernel,
        grid=(n // tm, nb),
        in_specs=[pl.BlockSpec((tm, D_MODEL), lambda i, j: (i, 0)), wspec(0), wspec(1), wspec(0), wspec(1)],
        out_specs=[ospec] * 4,
        out_shape=[sds] * 4,
        compiler_params=pltpu.CompilerParams(
            dimension_semantics=("arbitrary", "arbitrary"), vmem_limit_bytes=VMEM_LIMIT),
        name="b_proj",
    )(x1b, kv_w, kv_w, b_w_in, b_w_in)


def _sb_kernel(q_ref, k_ref, v_ref, zg_ref, o_ref):
    seq = q_ref.shape[0]
    T = SB_TILE
    rr = lax.broadcasted_iota(jnp.int32, (T, T), 0)
    cc = lax.broadcasted_iota(jnp.int32, (T, T), 1)
    below = cc < rr
    suffix = jnp.where(rr > cc, 1.0, 0.0).astype(BF16)
    ones_c = jnp.ones((T, LANES), BF16)

    def tile(q, k0, carry, acc, masked):
        k = k_ref[pl.ds(k0, T), :]
        v = v_ref[pl.ds(k0, T), :]
        z = _dot_nt(q, k)
        sp = _softplus(z)
        log_beta = z - sp
        log_1mb = -sp
        if masked:
            log_1mb = jnp.where(below, log_1mb, 0.0)
        lb = log_1mb.astype(BF16)
        between = _dot(lb, suffix) + jnp.concatenate([carry, carry], axis=1)
        a = jnp.exp(log_beta + between)
        if masked:
            a = jnp.where(below, a, 0.0)
        acc = acc + _dot(a.astype(BF16), v)
        carry = carry + _dot(lb, ones_c)
        return carry, acc

    def qblock(qi, c):
        q0 = pl.multiple_of(qi * T, T)
        q = q_ref[pl.ds(q0, T), :]
        zeros = jnp.zeros((T, LANES), F32)
        carry, acc = tile(q, q0, zeros, zeros, True)

        def body(i, ca):
            k0 = pl.multiple_of((qi - 1 - i) * T, T)
            return tile(q, k0, ca[0], ca[1], False)

        carry, acc = lax.fori_loop(0, qi, body, (carry, acc))
        o_ref[pl.ds(q0, T), :] = (acc * zg_ref[pl.ds(q0, T), :].astype(F32)).astype(BF16)
        return c

    lax.fori_loop(0, seq // T, qblock, 0)


def _stickbreak(q, k, v, zg, bsz, seq):
    n = bsz * seq
    spec = pl.BlockSpec((seq, DH_B), lambda b, h: (b, h))
    return pl.pallas_call(
        _sb_kernel,
        grid=(bsz, NH_B),
        in_specs=[spec] * 4,
        out_specs=spec,
        out_shape=jax.ShapeDtypeStruct((n, W_B), BF16),
        compiler_params=pltpu.CompilerParams(
            dimension_semantics=("arbitrary", "arbitrary"), vmem_limit_bytes=VMEM_LIMIT),
        name="stickbreak",
    )(q, k, v, zg)


def kernel(x, a_w_in, a_gate_b, a_conv_w, a_conv_b, a_head_g, a_w_out, a_ln_g, a_ln_b,
           kv_w, b_w_in, b_w_out, b_ln_g, b_ln_b):
    bsz, seq, d = x.shape
    assert d == D_MODEL and a_w_in.shape[0] == 1 and b_w_in.shape[0] == 1
    n = bsz * seq
    x2 = x.reshape(n, d)
    n_main = 2 * QK_A + 3 * V_A

    w_main = a_w_in[0, :, :n_main].astype(BF16)
    w_g = a_w_in[0, :, n_main:].astype(BF16)
    qk, v, gate, gates = _a_in_proj(x2, w_main, w_g, a_gate_b[0].reshape(1, -1))
    hg = _mlstm(qk, v, gate, gates, a_conv_w[0], a_conv_b[0].reshape(1, -1), a_head_g[0].reshape(1, -1),
                bsz, seq)
    x1, x1b = _out_ln(hg, x2, a_w_out[0].astype(BF16), a_ln_g[0].reshape(1, -1), a_ln_b[0].reshape(1, -1), True)

    k_sh, v_sh, q, zg = _b_proj(x1b, kv_w.astype(BF16), b_w_in[0].astype(BF16))
    att = _stickbreak(q, k_sh, v_sh, zg, bsz, seq)
    (out,) = _out_ln(att, x1, b_w_out[0].astype(BF16), b_ln_g[0].reshape(1, -1), b_ln_b[0].reshape(1, -1), False)
    return out.reshape(bsz, seq, d)
```

```python
import jax
import jax.numpy as jnp
from jax import lax
from jax.experimental import pallas as pl
from jax.experimental.pallas import tpu as pltpu

F32 = jnp.float32
BF16 = jnp.bfloat16

D_MODEL = 2048
NH_A = 8
DK_A = 128
DV_A = 256
QK_A = NH_A * DK_A
V_A = NH_A * DV_A
CONV_A = 4
NH_B = 16
DH_B = 128
W_B = NH_B * DH_B
DEPTH = 2
ALPHA = (2.0 * DEPTH) ** 0.25
LN_EPS = 1e-5

LANES = 128
VMEM_LIMIT = 56 * 1024 * 1024

MLSTM_CHUNK = 128
MLSTM_HEADS = 4
CONV_TILE = 128
SB_TILE = 256
SB_HEADS = 4
SB_DEAD_LOG = -104.0
NEG_BIG = -1e30


def _dot(a, b):
    return jnp.dot(a, b, preferred_element_type=F32)


def _dot_nt(a, b):
    return lax.dot_general(a, b, (((1,), (1,)), ((), ())), preferred_element_type=F32)


def _dot_tn(a, b):
    return lax.dot_general(a, b, (((0,), (0,)), ((), ())), preferred_element_type=F32)


def _split_bf16(x):
    hi = x.astype(BF16)
    lo = (x - hi.astype(F32)).astype(BF16)
    return hi, lo


def _a_in_kernel(x_ref, wqk_ref, wv_ref, wo_ref, wz_ref, wg_ref, gb_ref,
                 qk_ref, v_ref, gate_ref, gates_ref, xb_ref):
    @pl.when(pl.program_id(1) == 0)
    def _():
        xb = x_ref[...].astype(BF16)
        xb_ref[...] = xb
        g = _dot(xb, wg_ref[...]) + gb_ref[...]
        lane = lax.broadcasted_iota(jnp.int32, g.shape, 1)
        log_f = jnp.minimum(g, 0.0) - jnp.log1p(jnp.exp(-jnp.abs(g)))
        gates_ref[...] = jnp.where(lane < NH_A, g, log_f)

    xb = xb_ref[...]
    qk_ref[...] = _dot(xb, wqk_ref[...]).astype(BF16)
    v_ref[...] = _dot(xb, wv_ref[...]).astype(BF16)
    o = _dot(xb, wo_ref[...])
    z = _dot(xb, wz_ref[...])
    gate_ref[...] = (jax.nn.sigmoid(o) * z * jax.nn.sigmoid(z)).astype(BF16)


def _a_in_proj(x2, w_main, w_g, gate_b, tm=1024, tn=256):
    n = x2.shape[0]
    nb = D_MODEL // tn
    wspec = lambda g: pl.BlockSpec((D_MODEL, tn), lambda i, j, g=g: (0, g * nb + j))
    ospec = pl.BlockSpec((tm, tn), lambda i, j: (i, j))
    return pl.pallas_call(
        _a_in_kernel,
        grid=(n // tm, nb),
        in_specs=[pl.BlockSpec((tm, D_MODEL), lambda i, j: (i, 0)),
                  wspec(0), wspec(1), wspec(2), wspec(3),
                  pl.BlockSpec((D_MODEL, 2 * NH_A), lambda i, j: (0, 0)),
                  pl.BlockSpec((1, 2 * NH_A), lambda i, j: (0, 0))],
        out_specs=[ospec, ospec, ospec, pl.BlockSpec((tm, 2 * NH_A), lambda i, j: (i, 0))],
        out_shape=[jax.ShapeDtypeStruct((n, 2 * QK_A), BF16),
                   jax.ShapeDtypeStruct((n, V_A), BF16),
                   jax.ShapeDtypeStruct((n, V_A), BF16),
                   jax.ShapeDtypeStruct((n, 2 * NH_A), F32)],
        scratch_shapes=[pltpu.VMEM((tm, D_MODEL), BF16)],
        compiler_params=pltpu.CompilerParams(
            dimension_semantics=("arbitrary", "arbitrary"), vmem_limit_bytes=VMEM_LIMIT),
        name="a_in_proj",
    )(x2, w_main, w_main, w_main, w_main, w_g, gate_b)


def _mlstm_kernel(qp_ref, kp_ref, v_ref, gate_ref, gates_ref, cwq_ref, cbq_ref, cwk_ref, cbk_ref, hg_ref,
                  out_ref, xf_ref, q_s, k_s, st_ref, m_ref):
    seq = qp_ref.shape[0]
    L = MLSTM_CHUNK
    HP = MLSTM_HEADS
    head0 = pl.program_id(1) * HP
    width = HP * DK_A

    pad = 8
    for src, cw, cb, dst, scale in ((qp_ref, cwq_ref, cbq_ref, q_s, 1.0),
                                    (kp_ref, cwk_ref, cbk_ref, k_s, DK_A ** -0.5)):
        xf_ref[0:pad, :] = jnp.zeros((pad, width), F32)
        xf_ref[pad:, :] = src[...].astype(F32)
        w = cw[...]
        bias = cb[...]

        def conv_tile(t, carry, w=w, bias=bias, dst=dst, scale=scale):
            r0 = pl.multiple_of(t * CONV_TILE, CONV_TILE)
            acc = jnp.broadcast_to(bias, (CONV_TILE, width))
            win = xf_ref[pl.ds(r0, CONV_TILE + pad), :]
            for kk in range(CONV_A):
                off = pad - (CONV_A - 1) + kk
                acc = acc + w[kk:kk + 1, :] * win[off:off + CONV_TILE, :]
            y = acc * jax.nn.sigmoid(acc)
            if scale != 1.0:
                y = y * scale
            dst[pl.ds(r0, CONV_TILE), :] = y.astype(BF16)
            return carry

        lax.fori_loop(0, seq // CONV_TILE, conv_tile, 0)

    st_ref[...] = jnp.zeros(st_ref.shape, F32)
    m_ref[...] = jnp.zeros(m_ref.shape, F32)

    ng = 2 * NH_A
    sel_r = lax.broadcasted_iota(jnp.int32, (ng, LANES), 0)
    rr = lax.broadcasted_iota(jnp.int32, (L, L), 0)
    cc = lax.broadcasted_iota(jnp.int32, (L, L), 1)
    causal = cc <= rr
    tri = jnp.where(causal, 1.0, 0.0).astype(BF16)
    ones_v = jnp.ones((L, LANES), BF16)

    def head_chunk(hh, r0, g_hi, g_lo):
        head = head0 + hh
        oh_i = jnp.where(sel_r == head, 1.0, 0.0).astype(BF16)
        oh_f = jnp.where(sel_r == head + NH_A, 1.0, 0.0).astype(BF16)
        qk_lanes = slice(hh * DK_A, (hh + 1) * DK_A)
        v_lanes = slice(hh * DV_A, (hh + 1) * DV_A)
        q = q_s[pl.ds(r0, L), qk_lanes]
        k = k_s[pl.ds(r0, L), qk_lanes]
        vext = jnp.concatenate([v_ref[pl.ds(r0, L), v_lanes], ones_v], axis=1)
        ig = _dot(g_hi, oh_i) + _dot(g_lo, oh_i)
        lf = _dot(g_hi, oh_f) + _dot(g_lo, oh_f)
        lf_hi, lf_lo = _split_bf16(lf)
        bcum = _dot(tri, lf_hi) + _dot(tri, lf_lo)
        gtot = bcum[L - 1:L, :]
        m_st = m_ref[hh]

        a_row = jnp.transpose(ig - bcum)
        dmat = jnp.where(causal, bcum + a_row, NEG_BIG)
        inter = bcum + m_st
        m_q = jnp.maximum(inter, jnp.max(dmat, axis=1, keepdims=True))
        p = jnp.exp(dmat - m_q)
        scores = (_dot_nt(q, k) * p).astype(BF16)
        intra = _dot(scores, vext)
        st = st_ref[hh]
        carried = _dot(q, st.astype(BF16))
        w_inter = jnp.exp(inter - m_q)
        tot = intra + jnp.concatenate([w_inter] * 3, axis=1) * carried
        num = tot[:, :DV_A]
        den = tot[:, DV_A:]
        inv = 1.0 / jnp.maximum(jnp.abs(den), jnp.exp(-m_q))
        hh_out = num * jnp.concatenate([inv, inv], axis=1)

        mu = jnp.mean(hh_out, axis=1, keepdims=True)
        hc = hh_out - mu
        var = jnp.mean(hc * hc, axis=1, keepdims=True)
        hn = hc * lax.rsqrt(var + LN_EPS)
        gate = gate_ref[pl.ds(r0, L), v_lanes].astype(F32)
        out_ref[pl.ds(r0, L), v_lanes] = (hn * hg_ref[:, v_lanes] * gate).astype(BF16)

        wlog = gtot - bcum + ig
        m_new = jnp.maximum(gtot + m_st, jnp.max(wlog, axis=0, keepdims=True))
        wk = jnp.exp(wlog - m_new)
        kw = (k.astype(F32) * wk).astype(BF16)
        decay = jnp.exp(gtot + m_st - m_new)
        st_ref[hh] = jnp.concatenate([decay] * 3, axis=1) * st + _dot_tn(kw, vext)
        m_ref[hh] = m_new

    def chunk(c, carry):
        r0 = pl.multiple_of(c * L, L)
        g_hi, g_lo = _split_bf16(gates_ref[pl.ds(r0, L), :])
        for hh in range(HP):
            head_chunk(hh, r0, g_hi, g_lo)
        return carry

    lax.fori_loop(0, seq // L, chunk, 0)


def _mlstm(qk, v, gate, gates, conv_w, conv_b, head_g, bsz, seq):
    n = bsz * seq
    hp = MLSTM_HEADS
    ngrp = NH_A // hp
    return pl.pallas_call(
        _mlstm_kernel,
        grid=(bsz, ngrp),
        in_specs=[pl.BlockSpec((seq, hp * DK_A), lambda b, h: (b, h)),
                  pl.BlockSpec((seq, hp * DK_A), lambda b, h: (b, ngrp + h)),
                  pl.BlockSpec((seq, hp * DV_A), lambda b, h: (b, h)),
                  pl.BlockSpec((seq, hp * DV_A), lambda b, h: (b, h)),
                  pl.BlockSpec((seq, 2 * NH_A), lambda b, h: (b, 0)),
                  pl.BlockSpec((CONV_A, hp * DK_A), lambda b, h: (0, h)),
                  pl.BlockSpec((1, hp * DK_A), lambda b, h: (0, h)),
                  pl.BlockSpec((CONV_A, hp * DK_A), lambda b, h: (0, ngrp + h)),
                  pl.BlockSpec((1, hp * DK_A), lambda b, h: (0, ngrp + h)),
                  pl.BlockSpec((1, hp * DV_A), lambda b, h: (0, h))],
        out_specs=pl.BlockSpec((seq, hp * DV_A), lambda b, h: (b, h)),
        out_shape=jax.ShapeDtypeStruct((n, V_A), BF16),
        scratch_shapes=[pltpu.VMEM((seq + 8, hp * DK_A), F32),
                        pltpu.VMEM((seq, hp * DK_A), BF16),
                        pltpu.VMEM((seq, hp * DK_A), BF16),
                        pltpu.VMEM((hp, DK_A, DV_A + LANES), F32),
                        pltpu.VMEM((hp, 1, LANES), F32)],
        compiler_params=pltpu.CompilerParams(
            dimension_semantics=("arbitrary", "arbitrary"), vmem_limit_bytes=VMEM_LIMIT),
        name="mlstm",
    )(qk, qk, v, gate, gates, conv_w, conv_b, conv_w, conv_b, head_g)


def _out_ln_kernel(h_ref, x_ref, w_ref, g_ref, b_ref, *out_refs):
    y = _dot(h_ref[...], w_ref[...])
    r = ALPHA * x_ref[...] + y
    mu = jnp.mean(r, axis=1, keepdims=True)
    rc = r - mu
    var = jnp.mean(rc * rc, axis=1, keepdims=True)
    out = rc * lax.rsqrt(var + LN_EPS) * g_ref[...] + b_ref[...]
    out_refs[0][...] = out
    if len(out_refs) > 1:
        out_refs[1][...] = out.astype(BF16)


def _out_ln(h, x2, w, ln_g, ln_b, with_bf16, tm=256):
    n = x2.shape[0]
    row = pl.BlockSpec((tm, D_MODEL), lambda i: (i, 0))
    vec = pl.BlockSpec((1, D_MODEL), lambda i: (0, 0))
    out_shape = [jax.ShapeDtypeStruct((n, D_MODEL), F32)]
    out_specs = [row]
    if with_bf16:
        out_shape.append(jax.ShapeDtypeStruct((n, D_MODEL), BF16))
        out_specs.append(row)
    return pl.pallas_call(
        _out_ln_kernel,
        grid=(n // tm,),
        in_specs=[row, row, pl.BlockSpec((D_MODEL, D_MODEL), lambda i: (0, 0)), vec, vec],
        out_specs=out_specs,
        out_shape=out_shape,
        compiler_params=pltpu.CompilerParams(
            dimension_semantics=("arbitrary",), vmem_limit_bytes=VMEM_LIMIT),
        name="out_ln",
    )(h, x2, w, ln_g, ln_b)


def _b_proj_kernel(x_ref, wk_ref, wv_ref, wq_ref, wz_ref, k_ref, v_ref, q_ref, zg_ref):
    xb = x_ref[...]
    k_ref[...] = _dot(xb, wk_ref[...]).astype(BF16)
    v_ref[...] = _dot(xb, wv_ref[...]).astype(BF16)
    q_ref[...] = (_dot(xb, wq_ref[...]) * (DH_B ** -0.5)).astype(BF16)
    z = _dot(xb, wz_ref[...])
    zg_ref[...] = (z * jax.nn.sigmoid(z)).astype(BF16)


def _b_proj(x1b, kv_w, b_w_in, tm=1024, tn=256):
    n = x1b.shape[0]
    nb = W_B // tn
    wspec = lambda g: pl.BlockSpec((D_MODEL, tn), lambda i, j, g=g: (0, g * nb + j))
    ospec = pl.BlockSpec((tm, tn), lambda i, j: (i, j))
    sds = jax.ShapeDtypeStruct((n, W_B), BF16)
    return pl.pallas_call(
        _b_proj_kernel,
        grid=(n // tm, nb),
        in_specs=[pl.BlockSpec((tm, D_MODEL), lambda i, j: (i, 0)), wspec(0), wspec(1), wspec(0), wspec(1)],
        out_specs=[ospec] * 4,
        out_shape=[sds] * 4,
        compiler_params=pltpu.CompilerParams(
            dimension_semantics=("arbitrary", "arbitrary"), vmem_limit_bytes=VMEM_LIMIT),
        name="b_proj",
    )(x1b, kv_w, kv_w, b_w_in, b_w_in)


def _sb_kernel(q_ref, k_ref, v_ref, zg_ref, o_ref):
    seq = q_ref.shape[0]
    T = SB_TILE
    HP = q_ref.shape[1] // DH_B
    rr = lax.broadcasted_iota(jnp.int32, (T, T), 0)
    cc = lax.broadcasted_iota(jnp.int32, (T, T), 1)
    below = cc < rr
    neg_suffix = jnp.where(rr > cc, -1.0, 0.0).astype(BF16)

    def tile(hh, q0, k0, carry, acc, masked):
        lanes = slice(hh * DH_B, (hh + 1) * DH_B)
        q = q_ref[pl.ds(q0, T), lanes]
        k = k_ref[pl.ds(k0, T), lanes]
        v = v_ref[pl.ds(k0, T), lanes]
        z = _dot_nt(q, k)
        sp = jnp.maximum(z, 0.0) + jnp.log(1.0 + jnp.exp(-jnp.abs(z)))
        log_beta = z - sp
        if masked:
            sp = jnp.where(below, sp, 0.0)
        between = _dot(sp.astype(BF16), neg_suffix) + carry
        a = jnp.exp(log_beta + between)
        if masked:
            a = jnp.where(below, a, 0.0)
        acc = acc + _dot(a.astype(BF16), v)
        carry = carry - jnp.sum(sp, axis=1, keepdims=True)
        return carry, acc

    def any_alive(carries):
        m = carries[0]
        for c in carries[1:]:
            m = jnp.maximum(m, c)
        return (jnp.max(m) > SB_DEAD_LOG).astype(jnp.int32)

    def qblock(qi, c):
        q0 = pl.multiple_of(qi * T, T)
        zc = jnp.zeros((T, 1), F32)
        za = jnp.zeros((T, DH_B), F32)
        first = [tile(hh, q0, q0, zc, za, True) for hh in range(HP)]
        carries = tuple(s[0] for s in first)
        accs = tuple(s[1] for s in first)

        def cond(s):
            return jnp.logical_and(s[0] >= 0, s[1] > 0)

        def body(s):
            kb, _, carries, accs = s
            k0 = pl.multiple_of(kb * T, T)
            new = [tile(hh, q0, k0, carries[hh], accs[hh], False) for hh in range(HP)]
            carries = tuple(t[0] for t in new)
            accs = tuple(t[1] for t in new)
            return kb - 1, any_alive(carries), carries, accs

        _, _, _, accs = lax.while_loop(cond, body, (qi - 1, any_alive(carries), carries, accs))
        for hh in range(HP):
            lanes = slice(hh * DH_B, (hh + 1) * DH_B)
            o_ref[pl.ds(q0, T), lanes] = (accs[hh] * zg_ref[pl.ds(q0, T), lanes].astype(F32)).astype(BF16)
        return c

    lax.fori_loop(0, seq // T, qblock, 0)


def _stickbreak(q, k, v, zg, bsz, seq):
    n = bsz * seq
    spec = pl.BlockSpec((seq, SB_HEADS * DH_B), lambda b, h: (b, h))
    return pl.pallas_call(
        _sb_kernel,
        grid=(bsz, NH_B // SB_HEADS),
        in_specs=[spec] * 4,
        out_specs=spec,
        out_shape=jax.ShapeDtypeStruct((n, W_B), BF16),
        compiler_params=pltpu.CompilerParams(
            dimension_semantics=("arbitrary", "arbitrary"), vmem_limit_bytes=VMEM_LIMIT),
        name="stickbreak",
    )(q, k, v, zg)


def kernel(x, a_w_in, a_gate_b, a_conv_w, a_conv_b, a_head_g, a_w_out, a_ln_g, a_ln_b,
           kv_w, b_w_in, b_w_out, b_ln_g, b_ln_b):
    bsz, seq, d = x.shape
    assert d == D_MODEL and a_w_in.shape[0] == 1 and b_w_in.shape[0] == 1
    n = bsz * seq
    x2 = x.reshape(n, d)
    n_main = 2 * QK_A + 3 * V_A

    w_main = a_w_in[0, :, :n_main].astype(BF16)
    w_g = a_w_in[0, :, n_main:].astype(BF16)
    qk, v, gate, gates = _a_in_proj(x2, w_main, w_g, a_gate_b[0].reshape(1, -1))
    hg = _mlstm(qk, v, gate, gates, a_conv_w[0], a_conv_b[0].reshape(1, -1), a_head_g[0].reshape(1, -1),
                bsz, seq)
    x1, x1b = _out_ln(hg, x2, a_w_out[0].astype(BF16), a_ln_g[0].reshape(1, -1), a_ln_b[0].reshape(1, -1), True)

    k_sh, v_sh, q, zg = _b_proj(x1b, kv_w.astype(BF16), b_w_in[0].astype(BF16))
    att = _stickbreak(q, k_sh, v_sh, zg, bsz, seq)
    (out,) = _out_ln(att, x1, b_w_out[0].astype(BF16), b_ln_g[0].reshape(1, -1), b_ln_b[0].reshape(1, -1), False)
    return out.reshape(bsz, seq, d)
```

```python
import functools

import jax
import jax.numpy as jnp
from jax import lax
from jax.experimental import pallas as pl
from jax.experimental.pallas import tpu as pltpu

F32 = jnp.float32
BF16 = jnp.bfloat16

D_MODEL = 2048
NH_A = 8
DK_A = 128
DV_A = 256
QK_A = NH_A * DK_A
V_A = NH_A * DV_A
CONV_A = 4
NH_B = 16
DH_B = 128
W_B = NH_B * DH_B
DEPTH = 2
ALPHA = (2.0 * DEPTH) ** 0.25
LN_EPS = 1e-5

LANES = 128
VMEM_LIMIT = 56 * 1024 * 1024

MLSTM_CHUNK = 128
MLSTM_HEADS = 4
CONV_TILE = 256
SB_TILE = 256
SB_HEADS = 4
SB_DEAD_LOG = -104.0
NEG_BIG = -1e30


def _dot(a, b):
    return jnp.dot(a, b, preferred_element_type=F32)


def _dot_nt(a, b):
    return lax.dot_general(a, b, (((1,), (1,)), ((), ())), preferred_element_type=F32)


def _dot_tn(a, b):
    return lax.dot_general(a, b, (((0,), (0,)), ((), ())), preferred_element_type=F32)


def _split_bf16(x):
    hi = x.astype(BF16)
    lo = (x - hi.astype(F32)).astype(BF16)
    return hi, lo


def _a_in_kernel(x_ref, xh_ref, wqk_ref, wv_ref, wo_ref, wz_ref, wg_ref, gb_ref, cw_ref, cb_ref,
                 qk_ref, v_ref, gate_ref, gates_ref, xb_ref, hb_ref, u_ref, *, tiles_per_seq):
    i = pl.program_id(0)
    j = pl.program_id(1)
    tm = x_ref.shape[0]
    halo = xh_ref.shape[0]

    @pl.when(j == 0)
    def _():
        xb = x_ref[...].astype(BF16)
        xb_ref[...] = xb
        keep = (i % tiles_per_seq != 0).astype(F32)
        hb_ref[...] = (xh_ref[...] * keep).astype(BF16)
        g = _dot(xb, wg_ref[...]) + gb_ref[...]
        lane = lax.broadcasted_iota(jnp.int32, g.shape, 1)
        log_f = jnp.minimum(g, 0.0) - jnp.log1p(jnp.exp(-jnp.abs(g)))
        gates_ref[...] = jnp.where(lane < NH_A, g, log_f)

    xb = xb_ref[...]
    wqk = wqk_ref[...]
    u_ref[0:halo, :] = _dot(hb_ref[...], wqk)
    u_ref[halo:, :] = _dot(xb, wqk)
    w = cw_ref[...]
    bias = cb_ref[...]
    scale = jnp.where(j >= pl.num_programs(1) // 2, DK_A ** -0.5, 1.0)
    for t in range(tm // CONV_TILE):
        acc = jnp.broadcast_to(bias, (CONV_TILE, bias.shape[1]))
        for kk in range(CONV_A):
            off = halo - (CONV_A - 1) + kk + t * CONV_TILE
            acc = acc + w[kk:kk + 1, :] * u_ref[off:off + CONV_TILE, :]
        qk_ref[t * CONV_TILE:(t + 1) * CONV_TILE, :] = (acc * jax.nn.sigmoid(acc) * scale).astype(BF16)

    v_ref[...] = _dot(xb, wv_ref[...]).astype(BF16)
    o = _dot(xb, wo_ref[...])
    z = _dot(xb, wz_ref[...])
    gate_ref[...] = (jax.nn.sigmoid(o) * z * jax.nn.sigmoid(z)).astype(BF16)


def _a_in_proj(x2, w_main, w_g, gate_b, conv_w, conv_b, seq, tm=1024, tn=256):
    n = x2.shape[0]
    nb = D_MODEL // tn
    halo = 16
    wspec = lambda g: pl.BlockSpec((D_MODEL, tn), lambda i, j, g=g: (0, g * nb + j))
    ospec = pl.BlockSpec((tm, tn), lambda i, j: (i, j))
    return pl.pallas_call(
        functools.partial(_a_in_kernel, tiles_per_seq=seq // tm),
        grid=(n // tm, nb),
        in_specs=[pl.BlockSpec((tm, D_MODEL), lambda i, j: (i, 0)),
                  pl.BlockSpec((halo, D_MODEL), lambda i, j: (jnp.maximum(i * (tm // halo) - 1, 0), 0)),
                  wspec(0), wspec(1), wspec(2), wspec(3),
                  pl.BlockSpec((D_MODEL, 2 * NH_A), lambda i, j: (0, 0)),
                  pl.BlockSpec((1, 2 * NH_A), lambda i, j: (0, 0)),
                  pl.BlockSpec((CONV_A, tn), lambda i, j: (0, j)),
                  pl.BlockSpec((1, tn), lambda i, j: (0, j))],
        out_specs=[ospec, ospec, ospec, pl.BlockSpec((tm, 2 * NH_A), lambda i, j: (i, 0))],
        out_shape=[jax.ShapeDtypeStruct((n, 2 * QK_A), BF16),
                   jax.ShapeDtypeStruct((n, V_A), BF16),
                   jax.ShapeDtypeStruct((n, V_A), BF16),
                   jax.ShapeDtypeStruct((n, 2 * NH_A), F32)],
        scratch_shapes=[pltpu.VMEM((tm, D_MODEL), BF16),
                        pltpu.VMEM((halo, D_MODEL), BF16),
                        pltpu.VMEM((tm + halo, tn), F32)],
        compiler_params=pltpu.CompilerParams(
            dimension_semantics=("arbitrary", "arbitrary"), vmem_limit_bytes=VMEM_LIMIT),
        name="a_in_proj",
    )(x2, x2, w_main, w_main, w_main, w_main, w_g, gate_b, conv_w, conv_b)


def _mlstm_kernel(q_s, k_s, v_ref, gate_ref, gates_ref, hg_ref, out_ref, st_ref, m_ref):
    seq = q_s.shape[0]
    L = MLSTM_CHUNK
    HP = MLSTM_HEADS
    head0 = pl.program_id(1) * HP

    st_ref[...] = jnp.zeros(st_ref.shape, F32)
    m_ref[...] = jnp.zeros(m_ref.shape, F32)

    ng = 2 * NH_A
    sel_r = lax.broadcasted_iota(jnp.int32, (2 * ng, LANES), 0) % ng
    rr = lax.broadcasted_iota(jnp.int32, (L, L), 0)
    cc = lax.broadcasted_iota(jnp.int32, (L, L), 1)
    causal = cc <= rr
    tri = jnp.where(causal, 1.0, 0.0).astype(BF16)
    tri2 = jnp.concatenate([tri, tri], axis=1)
    ones_v = jnp.ones((L, LANES), BF16)
    heads = range(HP)
    oh_i = [jnp.where(sel_r == head0 + h, 1.0, 0.0).astype(BF16) for h in heads]
    oh_f = [jnp.where(sel_r == head0 + h + NH_A, 1.0, 0.0).astype(BF16) for h in heads]
    qk_l = [slice(h * DK_A, (h + 1) * DK_A) for h in heads]
    v_l = [slice(h * DV_A, (h + 1) * DV_A) for h in heads]

    def chunk(c, carry):
        r0 = pl.multiple_of(c * L, L)
        g_hi, g_lo = _split_bf16(gates_ref[pl.ds(r0, L), :])
        g2 = jnp.concatenate([g_hi, g_lo], axis=1)
        qs = [q_s[pl.ds(r0, L), qk_l[h]] for h in heads]
        ks = [k_s[pl.ds(r0, L), qk_l[h]] for h in heads]
        vexts = [jnp.concatenate([v_ref[pl.ds(r0, L), v_l[h]], ones_v], axis=1) for h in heads]
        sts = [st_ref[h] for h in heads]
        m_sts = [m_ref[h] for h in heads]

        igs = [_dot(g2, oh_i[h]) for h in heads]
        lfs = [_dot(g2, oh_f[h]) for h in heads]
        qk = [_dot_nt(qs[h], ks[h]) for h in heads]
        carrieds = [_dot(qs[h], sts[h].astype(BF16)) for h in heads]
        bcums = []
        for h in heads:
            lf_hi, lf_lo = _split_bf16(lfs[h])
            bcums.append(_dot(tri2, jnp.concatenate([lf_hi, lf_lo], axis=0)))
        gtots = [bc[L - 1:L, :] for bc in bcums]
        a_rows = [jnp.transpose(igs[h] - bcums[h]) for h in heads]
        dmats = [jnp.where(causal, bcums[h] + a_rows[h], NEG_BIG) for h in heads]
        inters = [bcums[h] + m_sts[h] for h in heads]
        m_qs = [jnp.maximum(inters[h], jnp.max(dmats[h], axis=1, keepdims=True)) for h in heads]
        scores = [(qk[h] * jnp.exp(dmats[h] - m_qs[h])).astype(BF16) for h in heads]
        intras = [_dot(scores[h], vexts[h]) for h in heads]

        for h in heads:
            wlog = gtots[h] - bcums[h] + igs[h]
            m_new = jnp.maximum(gtots[h] + m_sts[h], jnp.max(wlog, axis=0, keepdims=True))
            kw = (ks[h].astype(F32) * jnp.exp(wlog - m_new)).astype(BF16)
            decay = jnp.exp(gtots[h] + m_sts[h] - m_new)
            st_ref[h] = jnp.concatenate([decay] * 3, axis=1) * sts[h] + _dot_tn(kw, vexts[h])
            m_ref[h] = m_new

        for h in heads:
            w_inter = jnp.exp(inters[h] - m_qs[h])
            tot = intras[h] + jnp.concatenate([w_inter] * 3, axis=1) * carrieds[h]
            inv = 1.0 / jnp.maximum(jnp.abs(tot[:, DV_A:]), jnp.exp(-m_qs[h]))
            hv = tot[:, :DV_A] * jnp.concatenate([inv, inv], axis=1)
            mu = jnp.mean(hv, axis=1, keepdims=True)
            hc = hv - mu
            var = jnp.mean(hc * hc, axis=1, keepdims=True)
            hn = hc * lax.rsqrt(var + LN_EPS)
            gate = gate_ref[pl.ds(r0, L), v_l[h]].astype(F32)
            out_ref[pl.ds(r0, L), v_l[h]] = (hn * hg_ref[:, v_l[h]] * gate).astype(BF16)
        return carry

    lax.fori_loop(0, seq // L, chunk, 0)


def _mlstm(qk, v, gate, gates, head_g, bsz, seq):
    n = bsz * seq
    hp = MLSTM_HEADS
    ngrp = NH_A // hp
    return pl.pallas_call(
        _mlstm_kernel,
        grid=(bsz, ngrp),
        in_specs=[pl.BlockSpec((seq, hp * DK_A), lambda b, h: (b, h)),
                  pl.BlockSpec((seq, hp * DK_A), lambda b, h: (b, ngrp + h)),
                  pl.BlockSpec((seq, hp * DV_A), lambda b, h: (b, h)),
                  pl.BlockSpec((seq, hp * DV_A), lambda b, h: (b, h)),
                  pl.BlockSpec((seq, 2 * NH_A), lambda b, h: (b, 0)),
                  pl.BlockSpec((1, hp * DV_A), lambda b, h: (0, h))],
        out_specs=pl.BlockSpec((seq, hp * DV_A), lambda b, h: (b, h)),
        out_shape=jax.ShapeDtypeStruct((n, V_A), BF16),
        scratch_shapes=[pltpu.VMEM((hp, DK_A, DV_A + LANES), F32),
                        pltpu.VMEM((hp, 1, LANES), F32)],
        compiler_params=pltpu.CompilerParams(
            dimension_semantics=("arbitrary", "arbitrary"), vmem_limit_bytes=VMEM_LIMIT),
        name="mlstm",
    )(qk, qk, v, gate, gates, head_g)


def _out_ln_kernel(h_ref, x_ref, w_ref, g_ref, b_ref, *out_refs):
    y = _dot(h_ref[...], w_ref[...])
    r = ALPHA * x_ref[...] + y
    mu = jnp.mean(r, axis=1, keepdims=True)
    rc = r - mu
    var = jnp.mean(rc * rc, axis=1, keepdims=True)
    out = rc * lax.rsqrt(var + LN_EPS) * g_ref[...] + b_ref[...]
    out_refs[0][...] = out
    if len(out_refs) > 1:
        out_refs[1][...] = out.astype(BF16)


def _out_ln(h, x2, w, ln_g, ln_b, with_bf16, tm=256):
    n = x2.shape[0]
    row = pl.BlockSpec((tm, D_MODEL), lambda i: (i, 0))
    vec = pl.BlockSpec((1, D_MODEL), lambda i: (0, 0))
    out_shape = [jax.ShapeDtypeStruct((n, D_MODEL), F32)]
    out_specs = [row]
    if with_bf16:
        out_shape.append(jax.ShapeDtypeStruct((n, D_MODEL), BF16))
        out_specs.append(row)
    return pl.pallas_call(
        _out_ln_kernel,
        grid=(n // tm,),
        in_specs=[row, row, pl.BlockSpec((D_MODEL, D_MODEL), lambda i: (0, 0)), vec, vec],
        out_specs=out_specs,
        out_shape=out_shape,
        compiler_params=pltpu.CompilerParams(
            dimension_semantics=("arbitrary",), vmem_limit_bytes=VMEM_LIMIT),
        name="out_ln",
    )(h, x2, w, ln_g, ln_b)


def _b_proj_kernel(x_ref, wk_ref, wv_ref, wq_ref, wz_ref, k_ref, v_ref, q_ref, zg_ref):
    xb = x_ref[...]
    k_ref[...] = _dot(xb, wk_ref[...]).astype(BF16)
    v_ref[...] = _dot(xb, wv_ref[...]).astype(BF16)
    q_ref[...] = (_dot(xb, wq_ref[...]) * (DH_B ** -0.5)).astype(BF16)
    z = _dot(xb, wz_ref[...])
    zg_ref[...] = (z * jax.nn.sigmoid(z)).astype(BF16)


def _b_proj(x1b, kv_w, b_w_in, tm=1024, tn=256):
    n = x1b.shape[0]
    nb = W_B // tn
    wspec = lambda g: pl.BlockSpec((D_MODEL, tn), lambda i, j, g=g: (0, g * nb + j))
    ospec = pl.BlockSpec((tm, tn), lambda i, j: (i, j))
    sds = jax.ShapeDtypeStruct((n, W_B), BF16)
    return pl.pallas_call(
        _b_proj_kernel,
        grid=(n // tm, nb),
        in_specs=[pl.BlockSpec((tm, D_MODEL), lambda i, j: (i, 0)), wspec(0), wspec(1), wspec(0), wspec(1)],
        out_specs=[ospec] * 4,
        out_shape=[sds] * 4,
        compiler_params=pltpu.CompilerParams(
            dimension_semantics=("arbitrary", "arbitrary"), vmem_limit_bytes=VMEM_LIMIT),
        name="b_proj",
    )(x1b, kv_w, kv_w, b_w_in, b_w_in)


def _sb_kernel(q_ref, k_ref, v_ref, zg_ref, o_ref):
    seq = q_ref.shape[0]
    T = SB_TILE
    HP = q_ref.shape[1] // DH_B
    rr = lax.broadcasted_iota(jnp.int32, (T, T), 0)
    cc = lax.broadcasted_iota(jnp.int32, (T, T), 1)
    below = cc < rr
    neg_suffix = jnp.where(rr > cc, -1.0, 0.0).astype(BF16)

    def tiles(q0, k0, carries, accs, masked):
        lanes = [slice(hh * DH_B, (hh + 1) * DH_B) for hh in range(HP)]
        zs = [_dot_nt(q_ref[pl.ds(q0, T), ln], k_ref[pl.ds(k0, T), ln]) for ln in lanes]
        sps, log_betas = [], []
        for z in zs:
            sp = jnp.maximum(z, 0.0) + jnp.log(1.0 + jnp.exp(-jnp.abs(z)))
            log_betas.append(z - sp)
            sps.append(jnp.where(below, sp, 0.0) if masked else sp)
        betweens = [_dot(sp.astype(BF16), neg_suffix) for sp in sps]
        probs = []
        for log_beta, between, carry in zip(log_betas, betweens, carries):
            a = jnp.exp(log_beta + between + carry)
            probs.append((jnp.where(below, a, 0.0) if masked else a).astype(BF16))
        accs = tuple(acc + _dot(a, v_ref[pl.ds(k0, T), ln]) for acc, a, ln in zip(accs, probs, lanes))
        carries = tuple(carry - jnp.sum(sp, axis=1, keepdims=True) for carry, sp in zip(carries, sps))
        return carries, accs

    def any_alive(carries):
        m = carries[0]
        for c in carries[1:]:
            m = jnp.maximum(m, c)
        return (jnp.max(m) > SB_DEAD_LOG).astype(jnp.int32)

    def qblock(qi, c):
        q0 = pl.multiple_of(qi * T, T)
        zc = (jnp.zeros((T, 1), F32),) * HP
        za = (jnp.zeros((T, DH_B), F32),) * HP
        carries, accs = tiles(q0, q0, zc, za, True)

        def cond(s):
            return jnp.logical_and(s[0] >= 0, s[1] > 0)

        def body(s):
            kb, _, carries, accs = s
            carries, accs = tiles(q0, pl.multiple_of(kb * T, T), carries, accs, False)
            return kb - 1, any_alive(carries), carries, accs

        _, _, _, accs = lax.while_loop(cond, body, (qi - 1, any_alive(carries), carries, accs))
        for hh in range(HP):
            lanes = slice(hh * DH_B, (hh + 1) * DH_B)
            o_ref[pl.ds(q0, T), lanes] = (accs[hh] * zg_ref[pl.ds(q0, T), lanes].astype(F32)).astype(BF16)
        return c

    lax.fori_loop(0, seq // T, qblock, 0)


def _stickbreak(q, k, v, zg, bsz, seq):
    n = bsz * seq
    spec = pl.BlockSpec((seq, SB_HEADS * DH_B), lambda b, h: (b, h))
    return pl.pallas_call(
        _sb_kernel,
        grid=(bsz, NH_B // SB_HEADS),
        in_specs=[spec] * 4,
        out_specs=spec,
        out_shape=jax.ShapeDtypeStruct((n, W_B), BF16),
        compiler_params=pltpu.CompilerParams(
            dimension_semantics=("arbitrary", "arbitrary"), vmem_limit_bytes=VMEM_LIMIT),
        name="stickbreak",
    )(q, k, v, zg)


def kernel(x, a_w_in, a_gate_b, a_conv_w, a_conv_b, a_head_g, a_w_out, a_ln_g, a_ln_b,
           kv_w, b_w_in, b_w_out, b_ln_g, b_ln_b):
    bsz, seq, d = x.shape
    assert d == D_MODEL and a_w_in.shape[0] == 1 and b_w_in.shape[0] == 1
    n = bsz * seq
    x2 = x.reshape(n, d)
    n_main = 2 * QK_A + 3 * V_A

    w_main = a_w_in[0, :, :n_main].astype(BF16)
    w_g = a_w_in[0, :, n_main:].astype(BF16)
    qk, v, gate, gates = _a_in_proj(x2, w_main, w_g, a_gate_b[0].reshape(1, -1),
                                    a_conv_w[0], a_conv_b[0].reshape(1, -1), seq)
    hg = _mlstm(qk, v, gate, gates, a_head_g[0].reshape(1, -1), bsz, seq)
    x1, x1b = _out_ln(hg, x2, a_w_out[0].astype(BF16), a_ln_g[0].reshape(1, -1), a_ln_b[0].reshape(1, -1), True)

    k_sh, v_sh, q, zg = _b_proj(x1b, kv_w.astype(BF16), b_w_in[0].astype(BF16))
    att = _stickbreak(q, k_sh, v_sh, zg, bsz, seq)
    (out,) = _out_ln(att, x1, b_w_out[0].astype(BF16), b_ln_g[0].reshape(1, -1), b_ln_b[0].reshape(1, -1), False)
    return out.reshape(bsz, seq, d)
```

```python
import functools

import jax
import jax.numpy as jnp
from jax import lax
from jax.experimental import pallas as pl
from jax.experimental.pallas import tpu as pltpu

F32 = jnp.float32
BF16 = jnp.bfloat16

D_MODEL = 2048
NH_A = 8
DK_A = 128
DV_A = 256
QK_A = NH_A * DK_A
V_A = NH_A * DV_A
CONV_A = 4
NH_B = 16
DH_B = 128
W_B = NH_B * DH_B
DEPTH = 2
ALPHA = (2.0 * DEPTH) ** 0.25
LN_EPS = 1e-5

LANES = 128
VMEM_LIMIT = 56 * 1024 * 1024

MLSTM_CHUNK = 128
MLSTM_HEADS = 4
ROW_SUB = 256
LN_COL_CHUNK = 512
SB_TILE = 256
SB_HEADS = 4
SB_DEAD_LOG = -104.0
NEG_BIG = -1e30


def _dot(a, b):
    return jnp.dot(a, b, preferred_element_type=F32)


def _dot_nt(a, b):
    return lax.dot_general(a, b, (((1,), (1,)), ((), ())), preferred_element_type=F32)


def _dot_tn(a, b):
    return lax.dot_general(a, b, (((0,), (0,)), ((), ())), preferred_element_type=F32)


def _split_bf16(x):
    hi = x.astype(BF16)
    lo = (x - hi.astype(F32)).astype(BF16)
    return hi, lo


def _a_in_kernel(x_ref, xh_ref, wqk_ref, wv_ref, wo_ref, wz_ref, wg_ref, gb_ref, cw_ref, cb_ref,
                 qk_ref, v_ref, gate_ref, gates_ref, xb_ref, hb_ref, u_ref, *, tiles_per_seq):
    i = pl.program_id(0)
    j = pl.program_id(1)
    tm = x_ref.shape[0]
    halo = xh_ref.shape[0]

    @pl.when(j == 0)
    def _():
        xb = x_ref[...].astype(BF16)
        xb_ref[...] = xb
        keep = (i % tiles_per_seq != 0).astype(F32)
        hb_ref[...] = (xh_ref[...] * keep).astype(BF16)
        wg = wg_ref[:, :2 * NH_A].astype(BF16)
        g = _dot(xb, wg) + gb_ref[...]
        lane = lax.broadcasted_iota(jnp.int32, g.shape, 1)
        log_f = jnp.minimum(g, 0.0) - jnp.log1p(jnp.exp(-jnp.abs(g)))
        gates_ref[...] = jnp.where(lane < NH_A, g, log_f)

    wqk = wqk_ref[...].astype(BF16)
    wv = wv_ref[...].astype(BF16)
    wo = wo_ref[...].astype(BF16)
    wz = wz_ref[...].astype(BF16)
    w = cw_ref[...]
    bias = cb_ref[...]
    scale = jnp.where(j >= pl.num_programs(1) // 2, DK_A ** -0.5, 1.0)
    u_ref[0:halo, :] = _dot(hb_ref[...], wqk)
    for t in range(tm // ROW_SUB):
        rows = slice(t * ROW_SUB, (t + 1) * ROW_SUB)
        xs = xb_ref[rows, :]
        u_ref[halo + t * ROW_SUB:halo + (t + 1) * ROW_SUB, :] = _dot(xs, wqk)
        v_ref[rows, :] = _dot(xs, wv).astype(BF16)
        o = _dot(xs, wo)
        z = _dot(xs, wz)
        gate_ref[rows, :] = (jax.nn.sigmoid(o) * z * jax.nn.sigmoid(z)).astype(BF16)
        acc = jnp.broadcast_to(bias, (ROW_SUB, bias.shape[1]))
        for kk in range(CONV_A):
            off = halo - (CONV_A - 1) + kk + t * ROW_SUB
            acc = acc + w[kk:kk + 1, :] * u_ref[off:off + ROW_SUB, :]
        qk_ref[rows, :] = (acc * jax.nn.sigmoid(acc) * scale).astype(BF16)


def _a_in_proj(x2, w_in, gate_b, conv_w, conv_b, seq, tm=1024, tn=256):
    n = x2.shape[0]
    nb = D_MODEL // tn
    halo = 16
    n_main = 2 * QK_A + 3 * V_A
    wspec = lambda g: pl.BlockSpec((None, D_MODEL, tn), lambda i, j, g=g: (0, 0, g * nb + j))
    ospec = pl.BlockSpec((tm, tn), lambda i, j: (i, j))
    return pl.pallas_call(
        functools.partial(_a_in_kernel, tiles_per_seq=seq // tm),
        grid=(n // tm, nb),
        in_specs=[pl.BlockSpec((tm, D_MODEL), lambda i, j: (i, 0)),
                  pl.BlockSpec((halo, D_MODEL), lambda i, j: (jnp.maximum(i * (tm // halo) - 1, 0), 0)),
                  wspec(0), wspec(1), wspec(2), wspec(3),
                  pl.BlockSpec((None, D_MODEL, LANES), lambda i, j: (0, 0, n_main // LANES)),
                  pl.BlockSpec((1, 2 * NH_A), lambda i, j: (0, 0)),
                  pl.BlockSpec((CONV_A, tn), lambda i, j: (0, j)),
                  pl.BlockSpec((1, tn), lambda i, j: (0, j))],
        out_specs=[ospec, ospec, ospec, pl.BlockSpec((tm, 2 * NH_A), lambda i, j: (i, 0))],
        out_shape=[jax.ShapeDtypeStruct((n, 2 * QK_A), BF16),
                   jax.ShapeDtypeStruct((n, V_A), BF16),
                   jax.ShapeDtypeStruct((n, V_A), BF16),
                   jax.ShapeDtypeStruct((n, 2 * NH_A), F32)],
        scratch_shapes=[pltpu.VMEM((tm, D_MODEL), BF16),
                        pltpu.VMEM((halo, D_MODEL), BF16),
                        pltpu.VMEM((tm + halo, tn), F32)],
        compiler_params=pltpu.CompilerParams(
            dimension_semantics=("arbitrary", "arbitrary"), vmem_limit_bytes=VMEM_LIMIT),
        name="a_in_proj",
    )(x2, x2, w_in, w_in, w_in, w_in, w_in, gate_b, conv_w, conv_b)


def _mlstm_kernel(q_s, k_s, v_ref, gate_ref, gates_ref, hg_ref, out_ref, st_ref, m_ref):
    seq = q_s.shape[0]
    L = MLSTM_CHUNK
    HP = MLSTM_HEADS
    head0 = pl.program_id(1) * HP

    st_ref[...] = jnp.zeros(st_ref.shape, F32)
    m_ref[...] = jnp.zeros(m_ref.shape, F32)

    ng = 2 * NH_A
    sel_r = lax.broadcasted_iota(jnp.int32, (2 * ng, LANES), 0) % ng
    rr = lax.broadcasted_iota(jnp.int32, (L, L), 0)
    cc = lax.broadcasted_iota(jnp.int32, (L, L), 1)
    causal = cc <= rr
    tri = jnp.where(causal, 1.0, 0.0).astype(BF16)
    tri2 = jnp.concatenate([tri, tri], axis=1)
    ones_v = jnp.ones((L, LANES), BF16)
    heads = range(HP)
    oh_i = [jnp.where(sel_r == head0 + h, 1.0, 0.0).astype(BF16) for h in heads]
    oh_f = [jnp.where(sel_r == head0 + h + NH_A, 1.0, 0.0).astype(BF16) for h in heads]
    qk_l = [slice(h * DK_A, (h + 1) * DK_A) for h in heads]
    v_l = [slice(h * DV_A, (h + 1) * DV_A) for h in heads]

    def chunk(c, carry):
        r0 = pl.multiple_of(c * L, L)
        g_hi, g_lo = _split_bf16(gates_ref[pl.ds(r0, L), :])
        g2 = jnp.concatenate([g_hi, g_lo], axis=1)
        qs = [q_s[pl.ds(r0, L), qk_l[h]] for h in heads]
        ks = [k_s[pl.ds(r0, L), qk_l[h]] for h in heads]
        vexts = [jnp.concatenate([v_ref[pl.ds(r0, L), v_l[h]], ones_v], axis=1) for h in heads]
        sts = [st_ref[h] for h in heads]
        m_sts = [m_ref[h] for h in heads]

        igs = [_dot(g2, oh_i[h]) for h in heads]
        lfs = [_dot(g2, oh_f[h]) for h in heads]
        qk = [_dot_nt(qs[h], ks[h]) for h in heads]
        carrieds = [_dot(qs[h], sts[h].astype(BF16)) for h in heads]
        bcums = []
        for h in heads:
            lf_hi, lf_lo = _split_bf16(lfs[h])
            bcums.append(_dot(tri2, jnp.concatenate([lf_hi, lf_lo], axis=0)))
        gtots = [bc[L - 1:L, :] for bc in bcums]
        a_rows = [jnp.transpose(igs[h] - bcums[h]) for h in heads]
        dmats = [jnp.where(causal, bcums[h] + a_rows[h], NEG_BIG) for h in heads]
        inters = [bcums[h] + m_sts[h] for h in heads]
        m_qs = [jnp.maximum(inters[h], jnp.max(dmats[h], axis=1, keepdims=True)) for h in heads]
        scores = [(qk[h] * jnp.exp(dmats[h] - m_qs[h])).astype(BF16) for h in heads]
        intras = [_dot(scores[h], vexts[h]) for h in heads]

        for h in heads:
            wlog = gtots[h] - bcums[h] + igs[h]
            m_new = jnp.maximum(gtots[h] + m_sts[h], jnp.max(wlog, axis=0, keepdims=True))
            kw = (ks[h].astype(F32) * jnp.exp(wlog - m_new)).astype(BF16)
            decay = jnp.exp(gtots[h] + m_sts[h] - m_new)
            st_ref[h] = jnp.concatenate([decay] * 3, axis=1) * sts[h] + _dot_tn(kw, vexts[h])
            m_ref[h] = m_new

        for h in heads:
            w_inter = jnp.exp(inters[h] - m_qs[h])
            tot = intras[h] + jnp.concatenate([w_inter] * 3, axis=1) * carrieds[h]
            inv = 1.0 / jnp.maximum(jnp.abs(tot[:, DV_A:]), jnp.exp(-m_qs[h]))
            hv = tot[:, :DV_A] * jnp.concatenate([inv, inv], axis=1)
            mu = jnp.mean(hv, axis=1, keepdims=True)
            hc = hv - mu
            var = jnp.mean(hc * hc, axis=1, keepdims=True)
            hn = hc * lax.rsqrt(var + LN_EPS)
            gate = gate_ref[pl.ds(r0, L), v_l[h]].astype(F32)
            out_ref[pl.ds(r0, L), v_l[h]] = (hn * hg_ref[:, v_l[h]] * gate).astype(BF16)
        return carry

    lax.fori_loop(0, seq // L, chunk, 0)


def _mlstm(qk, v, gate, gates, head_g, bsz, seq):
    n = bsz * seq
    hp = MLSTM_HEADS
    ngrp = NH_A // hp
    return pl.pallas_call(
        _mlstm_kernel,
        grid=(bsz, ngrp),
        in_specs=[pl.BlockSpec((seq, hp * DK_A), lambda b, h: (b, h)),
                  pl.BlockSpec((seq, hp * DK_A), lambda b, h: (b, ngrp + h)),
                  pl.BlockSpec((seq, hp * DV_A), lambda b, h: (b, h)),
                  pl.BlockSpec((seq, hp * DV_A), lambda b, h: (b, h)),
                  pl.BlockSpec((seq, 2 * NH_A), lambda b, h: (b, 0)),
                  pl.BlockSpec((1, hp * DV_A), lambda b, h: (0, h))],
        out_specs=pl.BlockSpec((seq, hp * DV_A), lambda b, h: (b, h)),
        out_shape=jax.ShapeDtypeStruct((n, V_A), BF16),
        scratch_shapes=[pltpu.VMEM((hp, DK_A, DV_A + LANES), F32),
                        pltpu.VMEM((hp, 1, LANES), F32)],
        compiler_params=pltpu.CompilerParams(
            dimension_semantics=("arbitrary", "arbitrary"), vmem_limit_bytes=VMEM_LIMIT),
        name="mlstm",
    )(qk, qk, v, gate, gates, head_g)


def _out_ln_kernel(h_ref, x_ref, w_ref, g_ref, b_ref, *out_refs):
    tm, d = x_ref.shape
    inv_d = 1.0 / d
    for t in range(tm // ROW_SUB):
        rows = slice(t * ROW_SUB, (t + 1) * ROW_SUB)
        hs = h_ref[rows, :]
        s1 = jnp.zeros((ROW_SUB, LANES), F32)
        s2 = jnp.zeros((ROW_SUB, LANES), F32)
        for c in range(d // LN_COL_CHUNK):
            cols = slice(c * LN_COL_CHUNK, (c + 1) * LN_COL_CHUNK)
            r = ALPHA * x_ref[rows, cols] + _dot(hs, w_ref[:, cols])
            out_refs[0][rows, cols] = r
            for l in range(LN_COL_CHUNK // LANES):
                rl = r[:, l * LANES:(l + 1) * LANES]
                s1 = s1 + rl
                s2 = s2 + rl * rl
        mu = jnp.sum(s1, axis=1, keepdims=True) * inv_d
        var = jnp.sum(s2, axis=1, keepdims=True) * inv_d - mu * mu
        rstd = lax.rsqrt(var + LN_EPS)
        for c in range(d // LN_COL_CHUNK):
            cols = slice(c * LN_COL_CHUNK, (c + 1) * LN_COL_CHUNK)
            out = (out_refs[0][rows, cols] - mu) * rstd * g_ref[:, cols] + b_ref[:, cols]
            out_refs[0][rows, cols] = out
            if len(out_refs) > 1:
                out_refs[1][rows, cols] = out.astype(BF16)


def _out_ln(h, x2, w, ln_g, ln_b, with_bf16, tm=512):
    n = x2.shape[0]
    row = pl.BlockSpec((tm, D_MODEL), lambda i: (i, 0))
    vec = pl.BlockSpec((1, D_MODEL), lambda i: (0, 0))
    out_shape = [jax.ShapeDtypeStruct((n, D_MODEL), F32)]
    out_specs = [row]
    if with_bf16:
        out_shape.append(jax.ShapeDtypeStruct((n, D_MODEL), BF16))
        out_specs.append(row)
    return pl.pallas_call(
        _out_ln_kernel,
        grid=(n // tm,),
        in_specs=[row, row, pl.BlockSpec((D_MODEL, D_MODEL), lambda i: (0, 0)), vec, vec],
        out_specs=out_specs,
        out_shape=out_shape,
        compiler_params=pltpu.CompilerParams(
            dimension_semantics=("arbitrary",), vmem_limit_bytes=VMEM_LIMIT),
        name="out_ln",
    )(h, x2, w, ln_g, ln_b)


def _b_proj_kernel(x_ref, wk_ref, wv_ref, wq_ref, wz_ref, k_ref, v_ref, q_ref, zg_ref):
    wk = wk_ref[...].astype(BF16)
    wv = wv_ref[...].astype(BF16)
    wq = wq_ref[...].astype(BF16)
    wz = wz_ref[...].astype(BF16)
    for t in range(x_ref.shape[0] // ROW_SUB):
        rows = slice(t * ROW_SUB, (t + 1) * ROW_SUB)
        xs = x_ref[rows, :]
        k_ref[rows, :] = _dot(xs, wk).astype(BF16)
        v_ref[rows, :] = _dot(xs, wv).astype(BF16)
        q_ref[rows, :] = (_dot(xs, wq) * (DH_B ** -0.5)).astype(BF16)
        z = _dot(xs, wz)
        zg_ref[rows, :] = (z * jax.nn.sigmoid(z)).astype(BF16)


def _b_proj(x1b, kv_w, b_w_in, tm=1024, tn=256):
    n = x1b.shape[0]
    nb = W_B // tn
    wspec = lambda g: pl.BlockSpec((D_MODEL, tn), lambda i, j, g=g: (0, g * nb + j))
    wspec3 = lambda g: pl.BlockSpec((None, D_MODEL, tn), lambda i, j, g=g: (0, 0, g * nb + j))
    ospec = pl.BlockSpec((tm, tn), lambda i, j: (i, j))
    sds = jax.ShapeDtypeStruct((n, W_B), BF16)
    return pl.pallas_call(
        _b_proj_kernel,
        grid=(n // tm, nb),
        in_specs=[pl.BlockSpec((tm, D_MODEL), lambda i, j: (i, 0)), wspec(0), wspec(1), wspec3(0), wspec3(1)],
        out_specs=[ospec] * 4,
        out_shape=[sds] * 4,
        compiler_params=pltpu.CompilerParams(
            dimension_semantics=("arbitrary", "arbitrary"), vmem_limit_bytes=VMEM_LIMIT),
        name="b_proj",
    )(x1b, kv_w, kv_w, b_w_in, b_w_in)


def _sb_kernel(q_ref, k_ref, v_ref, zg_ref, o_ref):
    seq = q_ref.shape[0]
    T = SB_TILE
    HP = q_ref.shape[1] // DH_B
    rr = lax.broadcasted_iota(jnp.int32, (T, T), 0)
    cc = lax.broadcasted_iota(jnp.int32, (T, T), 1)
    below = cc < rr
    neg_suffix = jnp.where(rr > cc, -1.0, 0.0).astype(BF16)

    def tiles(q0, k0, carries, accs, masked):
        lanes = [slice(hh * DH_B, (hh + 1) * DH_B) for hh in range(HP)]
        zs = [_dot_nt(q_ref[pl.ds(q0, T), ln], k_ref[pl.ds(k0, T), ln]) for ln in lanes]
        sps, log_betas = [], []
        for z in zs:
            sp = jnp.maximum(z, 0.0) + jnp.log(1.0 + jnp.exp(-jnp.abs(z)))
            log_betas.append(z - sp)
            sps.append(jnp.where(below, sp, 0.0) if masked else sp)
        betweens = [_dot(sp.astype(BF16), neg_suffix) for sp in sps]
        probs = []
        for log_beta, between, carry in zip(log_betas, betweens, carries):
            a = jnp.exp(log_beta + between + carry)
            probs.append((jnp.where(below, a, 0.0) if masked else a).astype(BF16))
        accs = tuple(acc + _dot(a, v_ref[pl.ds(k0, T), ln]) for acc, a, ln in zip(accs, probs, lanes))
        carries = tuple(carry - jnp.sum(sp, axis=1, keepdims=True) for carry, sp in zip(carries, sps))
        return carries, accs

    def any_alive(carries):
        m = carries[0]
        for c in carries[1:]:
            m = jnp.maximum(m, c)
        return (jnp.max(m) > SB_DEAD_LOG).astype(jnp.int32)

    def qblock(qi, c):
        q0 = pl.multiple_of(qi * T, T)
        zc = (jnp.zeros((T, 1), F32),) * HP
        za = (jnp.zeros((T, DH_B), F32),) * HP
        carries, accs = tiles(q0, q0, zc, za, True)

        def cond(s):
            return jnp.logical_and(s[0] >= 0, s[1] > 0)

        def body(s):
            kb, _, carries, accs = s
            carries, accs = tiles(q0, pl.multiple_of(kb * T, T), carries, accs, False)
            return kb - 1, any_alive(carries), carries, accs

        _, _, _, accs = lax.while_loop(cond, body, (qi - 1, any_alive(carries), carries, accs))
        for hh in range(HP):
            lanes = slice(hh * DH_B, (hh + 1) * DH_B)
            o_ref[pl.ds(q0, T), lanes] = (accs[hh] * zg_ref[pl.ds(q0, T), lanes].astype(F32)).astype(BF16)
        return c

    lax.fori_loop(0, seq // T, qblock, 0)


def _stickbreak(q, k, v, zg, bsz, seq):
    n = bsz * seq
    spec = pl.BlockSpec((seq, SB_HEADS * DH_B), lambda b, h: (b, h))
    return pl.pallas_call(
        _sb_kernel,
        grid=(bsz, NH_B // SB_HEADS),
        in_specs=[spec] * 4,
        out_specs=spec,
        out_shape=jax.ShapeDtypeStruct((n, W_B), BF16),
        compiler_params=pltpu.CompilerParams(
            dimension_semantics=("arbitrary", "arbitrary"), vmem_limit_bytes=VMEM_LIMIT),
        name="stickbreak",
    )(q, k, v, zg)


def kernel(x, a_w_in, a_gate_b, a_conv_w, a_conv_b, a_head_g, a_w_out, a_ln_g, a_ln_b,
           kv_w, b_w_in, b_w_out, b_ln_g, b_ln_b):
    bsz, seq, d = x.shape
    assert d == D_MODEL and a_w_in.shape[0] == 1 and b_w_in.shape[0] == 1
    n = bsz * seq
    x2 = x.reshape(n, d)

    qk, v, gate, gates = _a_in_proj(x2, a_w_in, a_gate_b[0].reshape(1, -1),
                                    a_conv_w[0], a_conv_b[0].reshape(1, -1), seq)
    hg = _mlstm(qk, v, gate, gates, a_head_g[0].reshape(1, -1), bsz, seq)
    x1, x1b = _out_ln(hg, x2, a_w_out[0].astype(BF16), a_ln_g[0].reshape(1, -1), a_ln_b[0].reshape(1, -1), True)

    k_sh, v_sh, q, zg = _b_proj(x1b, kv_w, b_w_in)
    att = _stickbreak(q, k_sh, v_sh, zg, bsz, seq)
    (out,) = _out_ln(att, x1, b_w_out[0].astype(BF16), b_ln_g[0].reshape(1, -1), b_ln_b[0].reshape(1, -1), False)
    return out.reshape(bsz, seq, d)
```

```python
import functools

import jax
import jax.numpy as jnp
from jax import lax
from jax.experimental import pallas as pl
from jax.experimental.pallas import tpu as pltpu

F32 = jnp.float32
BF16 = jnp.bfloat16

D_MODEL = 2048
NH_A = 8
DK_A = 128
DV_A = 256
QK_A = NH_A * DK_A
V_A = NH_A * DV_A
CONV_A = 4
NH_B = 16
DH_B = 128
W_B = NH_B * DH_B
DEPTH = 2
ALPHA = (2.0 * DEPTH) ** 0.25
LN_EPS = 1e-5

LANES = 128
VMEM_LIMIT = 56 * 1024 * 1024

MLSTM_CHUNK = 128
MLSTM_HEADS = 4
ROW_SUB = 256
LN_COL_CHUNK = 512
SB_TILE = 256
SB_HEADS = 4
SB_DEAD_LOG = -104.0
NEG_BIG = -1e30


def _dot(a, b):
    return jnp.dot(a, b, preferred_element_type=F32)


def _dot_nt(a, b):
    return lax.dot_general(a, b, (((1,), (1,)), ((), ())), preferred_element_type=F32)


def _dot_tn(a, b):
    return lax.dot_general(a, b, (((0,), (0,)), ((), ())), preferred_element_type=F32)


def _split_bf16(x):
    hi = x.astype(BF16)
    lo = (x - hi.astype(F32)).astype(BF16)
    return hi, lo


def _a_in_kernel(x_ref, xh_ref, wqk_ref, wv_ref, wo_ref, wz_ref, wg_ref, gb_ref, cw_ref, cb_ref,
                 qk_ref, v_ref, gate_ref, gates_ref, xb_ref, hb_ref, u_ref, *, tiles_per_seq):
    i = pl.program_id(0)
    j = pl.program_id(1)
    tm = x_ref.shape[0]
    halo = xh_ref.shape[0]

    @pl.when(j == 0)
    def _():
        xb = x_ref[...].astype(BF16)
        xb_ref[...] = xb
        keep = (i % tiles_per_seq != 0).astype(F32)
        hb_ref[...] = (xh_ref[...] * keep).astype(BF16)
        g = _dot_nt(xb, wg_ref[...].astype(BF16)) + gb_ref[...]
        lane = lax.broadcasted_iota(jnp.int32, g.shape, 1)
        log_f = jnp.minimum(g, 0.0) - jnp.log1p(jnp.exp(-jnp.abs(g)))
        gates_ref[...] = jnp.where(lane < NH_A, g, log_f)

    wqk = wqk_ref[...].astype(BF16)
    wv = wv_ref[...].astype(BF16)
    wo = wo_ref[...].astype(BF16)
    wz = wz_ref[...].astype(BF16)
    w = cw_ref[...]
    bias = cb_ref[...]
    scale = jnp.where(j >= pl.num_programs(1) // 2, DK_A ** -0.5, 1.0)
    u_ref[0:halo, :] = _dot_nt(hb_ref[...], wqk)
    for t in range(tm // ROW_SUB):
        rows = slice(t * ROW_SUB, (t + 1) * ROW_SUB)
        xs = xb_ref[rows, :]
        u_ref[halo + t * ROW_SUB:halo + (t + 1) * ROW_SUB, :] = _dot_nt(xs, wqk)
        v_ref[rows, :] = _dot_nt(xs, wv).astype(BF16)
        o = _dot_nt(xs, wo)
        z = _dot_nt(xs, wz)
        gate_ref[rows, :] = (jax.nn.sigmoid(o) * z * jax.nn.sigmoid(z)).astype(BF16)
        acc = jnp.broadcast_to(bias, (ROW_SUB, bias.shape[1]))
        for kk in range(CONV_A):
            off = halo - (CONV_A - 1) + kk + t * ROW_SUB
            acc = acc + w[kk:kk + 1, :] * u_ref[off:off + ROW_SUB, :]
        qk_ref[rows, :] = (acc * jax.nn.sigmoid(acc) * scale).astype(BF16)


def _a_in_proj(x2, w_in_t, gate_b, conv_w, conv_b, seq, tm=1024, tn=256):
    n = x2.shape[0]
    nb = D_MODEL // tn
    halo = 16
    n_main = 2 * QK_A + 3 * V_A
    wspec = lambda g: pl.BlockSpec((None, tn, D_MODEL), lambda i, j, g=g: (0, g * nb + j, 0))
    ospec = pl.BlockSpec((tm, tn), lambda i, j: (i, j))
    return pl.pallas_call(
        functools.partial(_a_in_kernel, tiles_per_seq=seq // tm),
        grid=(n // tm, nb),
        in_specs=[pl.BlockSpec((tm, D_MODEL), lambda i, j: (i, 0)),
                  pl.BlockSpec((halo, D_MODEL), lambda i, j: (jnp.maximum(i * (tm // halo) - 1, 0), 0)),
                  wspec(0), wspec(1), wspec(2), wspec(3),
                  pl.BlockSpec((None, 2 * NH_A, D_MODEL), lambda i, j: (0, n_main // (2 * NH_A), 0)),
                  pl.BlockSpec((1, 2 * NH_A), lambda i, j: (0, 0)),
                  pl.BlockSpec((CONV_A, tn), lambda i, j: (0, j)),
                  pl.BlockSpec((1, tn), lambda i, j: (0, j))],
        out_specs=[ospec, ospec, ospec, pl.BlockSpec((tm, 2 * NH_A), lambda i, j: (i, 0))],
        out_shape=[jax.ShapeDtypeStruct((n, 2 * QK_A), BF16),
                   jax.ShapeDtypeStruct((n, V_A), BF16),
                   jax.ShapeDtypeStruct((n, V_A), BF16),
                   jax.ShapeDtypeStruct((n, 2 * NH_A), F32)],
        scratch_shapes=[pltpu.VMEM((tm, D_MODEL), BF16),
                        pltpu.VMEM((halo, D_MODEL), BF16),
                        pltpu.VMEM((tm + halo, tn), F32)],
        compiler_params=pltpu.CompilerParams(
            dimension_semantics=("arbitrary", "arbitrary"), vmem_limit_bytes=VMEM_LIMIT),
        name="a_in_proj",
    )(x2, x2, w_in_t, w_in_t, w_in_t, w_in_t, w_in_t, gate_b, conv_w, conv_b)


def _mlstm_kernel(q_s, k_s, v_ref, gate_ref, gates_ref, hg_ref, out_ref, st_ref, m_ref):
    seq = q_s.shape[0]
    L = MLSTM_CHUNK
    HP = MLSTM_HEADS
    head0 = pl.program_id(1) * HP

    st_ref[...] = jnp.zeros(st_ref.shape, F32)
    m_ref[...] = jnp.zeros(m_ref.shape, F32)

    ng = 2 * NH_A
    sel_r = lax.broadcasted_iota(jnp.int32, (2 * ng, LANES), 0) % ng
    rr = lax.broadcasted_iota(jnp.int32, (L, L), 0)
    cc = lax.broadcasted_iota(jnp.int32, (L, L), 1)
    causal = cc <= rr
    tri = jnp.where(causal, 1.0, 0.0).astype(BF16)
    tri2 = jnp.concatenate([tri, tri], axis=1)
    ones_v = jnp.ones((L, LANES), BF16)
    heads = range(HP)
    oh_i = [jnp.where(sel_r == head0 + h, 1.0, 0.0).astype(BF16) for h in heads]
    oh_f = [jnp.where(sel_r == head0 + h + NH_A, 1.0, 0.0).astype(BF16) for h in heads]
    qk_l = [slice(h * DK_A, (h + 1) * DK_A) for h in heads]
    v_l = [slice(h * DV_A, (h + 1) * DV_A) for h in heads]

    def chunk(c, carry):
        r0 = pl.multiple_of(c * L, L)
        g_hi, g_lo = _split_bf16(gates_ref[pl.ds(r0, L), :])
        g2 = jnp.concatenate([g_hi, g_lo], axis=1)
        qs = [q_s[pl.ds(r0, L), qk_l[h]] for h in heads]
        ks = [k_s[pl.ds(r0, L), qk_l[h]] for h in heads]
        vexts = [jnp.concatenate([v_ref[pl.ds(r0, L), v_l[h]], ones_v], axis=1) for h in heads]
        sts = [st_ref[h] for h in heads]
        m_sts = [m_ref[h] for h in heads]

        igs = [_dot(g2, oh_i[h]) for h in heads]
        lfs = [_dot(g2, oh_f[h]) for h in heads]
        qk = [_dot_nt(qs[h], ks[h]) for h in heads]
        carrieds = [_dot(qs[h], sts[h].astype(BF16)) for h in heads]
        bcums = []
        for h in heads:
            lf_hi, lf_lo = _split_bf16(lfs[h])
            bcums.append(_dot(tri2, jnp.concatenate([lf_hi, lf_lo], axis=0)))
        gtots = [bc[L - 1:L, :] for bc in bcums]
        a_rows = [jnp.transpose(igs[h] - bcums[h]) for h in heads]
        dmats = [jnp.where(causal, bcums[h] + a_rows[h], NEG_BIG) for h in heads]
        inters = [bcums[h] + m_sts[h] for h in heads]
        m_qs = [jnp.maximum(inters[h], jnp.max(dmats[h], axis=1, keepdims=True)) for h in heads]
        scores = [(qk[h] * jnp.exp(dmats[h] - m_qs[h])).astype(BF16) for h in heads]
        intras = [_dot(scores[h], vexts[h]) for h in heads]

        for h in heads:
            wlog = gtots[h] - bcums[h] + igs[h]
            m_new = jnp.maximum(gtots[h] + m_sts[h], jnp.max(wlog, axis=0, keepdims=True))
            kw = (ks[h].astype(F32) * jnp.exp(wlog - m_new)).astype(BF16)
            decay = jnp.exp(gtots[h] + m_sts[h] - m_new)
            st_ref[h] = jnp.concatenate([decay] * 3, axis=1) * sts[h] + _dot_tn(kw, vexts[h])
            m_ref[h] = m_new

        for h in heads:
            w_inter = jnp.exp(inters[h] - m_qs[h])
            tot = intras[h] + jnp.concatenate([w_inter] * 3, axis=1) * carrieds[h]
            inv = 1.0 / jnp.maximum(jnp.abs(tot[:, DV_A:]), jnp.exp(-m_qs[h]))
            hv = tot[:, :DV_A] * jnp.concatenate([inv, inv], axis=1)
            mu = jnp.mean(hv, axis=1, keepdims=True)
            hc = hv - mu
            var = jnp.mean(hc * hc, axis=1, keepdims=True)
            hn = hc * lax.rsqrt(var + LN_EPS)
            gate = gate_ref[pl.ds(r0, L), v_l[h]].astype(F32)
            out_ref[pl.ds(r0, L), v_l[h]] = (hn * hg_ref[:, v_l[h]] * gate).astype(BF16)
        return carry

    lax.fori_loop(0, seq // L, chunk, 0)


def _mlstm(qk, v, gate, gates, head_g, bsz, seq):
    n = bsz * seq
    hp = MLSTM_HEADS
    ngrp = NH_A // hp
    return pl.pallas_call(
        _mlstm_kernel,
        grid=(bsz, ngrp),
        in_specs=[pl.BlockSpec((seq, hp * DK_A), lambda b, h: (b, h)),
                  pl.BlockSpec((seq, hp * DK_A), lambda b, h: (b, ngrp + h)),
                  pl.BlockSpec((seq, hp * DV_A), lambda b, h: (b, h)),
                  pl.BlockSpec((seq, hp * DV_A), lambda b, h: (b, h)),
                  pl.BlockSpec((seq, 2 * NH_A), lambda b, h: (b, 0)),
                  pl.BlockSpec((1, hp * DV_A), lambda b, h: (0, h))],
        out_specs=pl.BlockSpec((seq, hp * DV_A), lambda b, h: (b, h)),
        out_shape=jax.ShapeDtypeStruct((n, V_A), BF16),
        scratch_shapes=[pltpu.VMEM((hp, DK_A, DV_A + LANES), F32),
                        pltpu.VMEM((hp, 1, LANES), F32)],
        compiler_params=pltpu.CompilerParams(
            dimension_semantics=("arbitrary", "arbitrary"), vmem_limit_bytes=VMEM_LIMIT),
        name="mlstm",
    )(qk, qk, v, gate, gates, head_g)


def _out_ln_kernel(h_ref, x_ref, w_ref, g_ref, b_ref, *out_refs):
    tm, d = x_ref.shape
    inv_d = 1.0 / d
    for t in range(tm // ROW_SUB):
        rows = slice(t * ROW_SUB, (t + 1) * ROW_SUB)
        hs = h_ref[rows, :]
        s1 = jnp.zeros((ROW_SUB, LANES), F32)
        s2 = jnp.zeros((ROW_SUB, LANES), F32)
        for c in range(d // LN_COL_CHUNK):
            cols = slice(c * LN_COL_CHUNK, (c + 1) * LN_COL_CHUNK)
            r = ALPHA * x_ref[rows, cols] + _dot(hs, w_ref[:, cols])
            out_refs[0][rows, cols] = r
            for l in range(LN_COL_CHUNK // LANES):
                rl = r[:, l * LANES:(l + 1) * LANES]
                s1 = s1 + rl
                s2 = s2 + rl * rl
        mu = jnp.sum(s1, axis=1, keepdims=True) * inv_d
        var = jnp.sum(s2, axis=1, keepdims=True) * inv_d - mu * mu
        rstd = lax.rsqrt(var + LN_EPS)
        for c in range(d // LN_COL_CHUNK):
            cols = slice(c * LN_COL_CHUNK, (c + 1) * LN_COL_CHUNK)
            out = (out_refs[0][rows, cols] - mu) * rstd * g_ref[:, cols] + b_ref[:, cols]
            out_refs[0][rows, cols] = out
            if len(out_refs) > 1:
                out_refs[1][rows, cols] = out.astype(BF16)


def _out_ln(h, x2, w, ln_g, ln_b, with_bf16, tm=512):
    n = x2.shape[0]
    row = pl.BlockSpec((tm, D_MODEL), lambda i: (i, 0))
    vec = pl.BlockSpec((1, D_MODEL), lambda i: (0, 0))
    out_shape = [jax.ShapeDtypeStruct((n, D_MODEL), F32)]
    out_specs = [row]
    if with_bf16:
        out_shape.append(jax.ShapeDtypeStruct((n, D_MODEL), BF16))
        out_specs.append(row)
    return pl.pallas_call(
        _out_ln_kernel,
        grid=(n // tm,),
        in_specs=[row, row, pl.BlockSpec((D_MODEL, D_MODEL), lambda i: (0, 0)), vec, vec],
        out_specs=out_specs,
        out_shape=out_shape,
        compiler_params=pltpu.CompilerParams(
            dimension_semantics=("arbitrary",), vmem_limit_bytes=VMEM_LIMIT),
        name="out_ln",
    )(h, x2, w, ln_g, ln_b)


def _b_proj_kernel(x_ref, wk_ref, wv_ref, wq_ref, wz_ref, k_ref, v_ref, q_ref, zg_ref):
    wk = wk_ref[...].astype(BF16)
    wv = wv_ref[...].astype(BF16)
    wq = wq_ref[...].astype(BF16)
    wz = wz_ref[...].astype(BF16)
    for t in range(x_ref.shape[0] // ROW_SUB):
        rows = slice(t * ROW_SUB, (t + 1) * ROW_SUB)
        xs = x_ref[rows, :]
        k_ref[rows, :] = _dot(xs, wk).astype(BF16)
        v_ref[rows, :] = _dot(xs, wv).astype(BF16)
        q_ref[rows, :] = (_dot(xs, wq) * (DH_B ** -0.5)).astype(BF16)
        z = _dot(xs, wz)
        zg_ref[rows, :] = (z * jax.nn.sigmoid(z)).astype(BF16)


def _b_proj(x1b, kv_w, b_w_in, tm=1024, tn=256):
    n = x1b.shape[0]
    nb = W_B // tn
    wspec = lambda g: pl.BlockSpec((D_MODEL, tn), lambda i, j, g=g: (0, g * nb + j))
    wspec3 = lambda g: pl.BlockSpec((None, D_MODEL, tn), lambda i, j, g=g: (0, 0, g * nb + j))
    ospec = pl.BlockSpec((tm, tn), lambda i, j: (i, j))
    sds = jax.ShapeDtypeStruct((n, W_B), BF16)
    return pl.pallas_call(
        _b_proj_kernel,
        grid=(n // tm, nb),
        in_specs=[pl.BlockSpec((tm, D_MODEL), lambda i, j: (i, 0)), wspec(0), wspec(1), wspec3(0), wspec3(1)],
        out_specs=[ospec] * 4,
        out_shape=[sds] * 4,
        compiler_params=pltpu.CompilerParams(
            dimension_semantics=("arbitrary", "arbitrary"), vmem_limit_bytes=VMEM_LIMIT),
        name="b_proj",
    )(x1b, kv_w, kv_w, b_w_in, b_w_in)


def _sb_kernel(q_ref, k_ref, v_ref, zg_ref, o_ref):
    seq = q_ref.shape[0]
    T = SB_TILE
    HP = q_ref.shape[1] // DH_B
    rr = lax.broadcasted_iota(jnp.int32, (T, T), 0)
    cc = lax.broadcasted_iota(jnp.int32, (T, T), 1)
    below = cc < rr
    neg_suffix = jnp.where(rr > cc, -1.0, 0.0).astype(BF16)

    def tiles(q0, k0, carries, accs, masked):
        lanes = [slice(hh * DH_B, (hh + 1) * DH_B) for hh in range(HP)]
        zs = [_dot_nt(q_ref[pl.ds(q0, T), ln], k_ref[pl.ds(k0, T), ln]) for ln in lanes]
        sps, log_betas = [], []
        for z in zs:
            sp = jnp.maximum(z, 0.0) + jnp.log(1.0 + jnp.exp(-jnp.abs(z)))
            log_betas.append(z - sp)
            sps.append(jnp.where(below, sp, 0.0) if masked else sp)
        betweens = [_dot(sp.astype(BF16), neg_suffix) for sp in sps]
        probs = []
        for log_beta, between, carry in zip(log_betas, betweens, carries):
            a = jnp.exp(log_beta + between + carry)
            probs.append((jnp.where(below, a, 0.0) if masked else a).astype(BF16))
        accs = tuple(acc + _dot(a, v_ref[pl.ds(k0, T), ln]) for acc, a, ln in zip(accs, probs, lanes))
        carries = tuple(carry - jnp.sum(sp, axis=1, keepdims=True) for carry, sp in zip(carries, sps))
        return carries, accs

    def any_alive(carries):
        m = carries[0]
        for c in carries[1:]:
            m = jnp.maximum(m, c)
        return (jnp.max(m) > SB_DEAD_LOG).astype(jnp.int32)

    def qblock(qi, c):
        q0 = pl.multiple_of(qi * T, T)
        zc = (jnp.zeros((T, 1), F32),) * HP
        za = (jnp.zeros((T, DH_B), F32),) * HP
        carries, accs = tiles(q0, q0, zc, za, True)

        def cond(s):
            return jnp.logical_and(s[0] >= 0, s[1] > 0)

        def body(s):
            kb, _, carries, accs = s
            carries, accs = tiles(q0, pl.multiple_of(kb * T, T), carries, accs, False)
            return kb - 1, any_alive(carries), carries, accs

        _, _, _, accs = lax.while_loop(cond, body, (qi - 1, any_alive(carries), carries, accs))
        for hh in range(HP):
            lanes = slice(hh * DH_B, (hh + 1) * DH_B)
            o_ref[pl.ds(q0, T), lanes] = (accs[hh] * zg_ref[pl.ds(q0, T), lanes].astype(F32)).astype(BF16)
        return c

    lax.fori_loop(0, seq // T, qblock, 0)


def _stickbreak(q, k, v, zg, bsz, seq):
    n = bsz * seq
    spec = pl.BlockSpec((seq, SB_HEADS * DH_B), lambda b, h: (b, h))
    return pl.pallas_call(
        _sb_kernel,
        grid=(bsz, NH_B // SB_HEADS),
        in_specs=[spec] * 4,
        out_specs=spec,
        out_shape=jax.ShapeDtypeStruct((n, W_B), BF16),
        compiler_params=pltpu.CompilerParams(
            dimension_semantics=("arbitrary", "arbitrary"), vmem_limit_bytes=VMEM_LIMIT),
        name="stickbreak",
    )(q, k, v, zg)


def kernel(x, a_w_in, a_gate_b, a_conv_w, a_conv_b, a_head_g, a_w_out, a_ln_g, a_ln_b,
           kv_w, b_w_in, b_w_out, b_ln_g, b_ln_b):
    bsz, seq, d = x.shape
    assert d == D_MODEL and a_w_in.shape[0] == 1 and b_w_in.shape[0] == 1
    n = bsz * seq
    x2 = x.reshape(n, d)

    qk, v, gate, gates = _a_in_proj(x2, jnp.swapaxes(a_w_in, 1, 2), a_gate_b[0].reshape(1, -1),
                                    a_conv_w[0], a_conv_b[0].reshape(1, -1), seq)
    hg = _mlstm(qk, v, gate, gates, a_head_g[0].reshape(1, -1), bsz, seq)
    x1, x1b = _out_ln(hg, x2, a_w_out[0].astype(BF16), a_ln_g[0].reshape(1, -1), a_ln_b[0].reshape(1, -1), True)

    k_sh, v_sh, q, zg = _b_proj(x1b, kv_w, b_w_in)
    att = _stickbreak(q, k_sh, v_sh, zg, bsz, seq)
    (out,) = _out_ln(att, x1, b_w_out[0].astype(BF16), b_ln_g[0].reshape(1, -1), b_ln_b[0].reshape(1, -1), False)
    return out.reshape(bsz, seq, d)
```

```python
import functools

import jax
import jax.numpy as jnp
from jax import lax
from jax.experimental import pallas as pl
from jax.experimental.pallas import tpu as pltpu

F32 = jnp.float32
BF16 = jnp.bfloat16

D_MODEL = 2048
NH_A = 8
DK_A = 128
DV_A = 256
QK_A = NH_A * DK_A
V_A = NH_A * DV_A
CONV_A = 4
NH_B = 16
DH_B = 128
W_B = NH_B * DH_B
DEPTH = 2
ALPHA = (2.0 * DEPTH) ** 0.25
LN_EPS = 1e-5

LANES = 128
VMEM_LIMIT = 56 * 1024 * 1024

MLSTM_CHUNK = 128
MLSTM_HEADS = 4
ROW_SUB = 256
LN_COL_CHUNK = 512
SB_TILE = 256
SB_HEADS = 4
SB_DEAD_LOG = -104.0
NEG_BIG = -1e30


def _dot(a, b):
    return jnp.dot(a, b, preferred_element_type=F32)


def _dot_nt(a, b):
    return lax.dot_general(a, b, (((1,), (1,)), ((), ())), preferred_element_type=F32)


def _dot_tn(a, b):
    return lax.dot_general(a, b, (((0,), (0,)), ((), ())), preferred_element_type=F32)


def _split_bf16(x):
    hi = x.astype(BF16)
    lo = (x - hi.astype(F32)).astype(BF16)
    return hi, lo


def _a_in_kernel(x_ref, xh_ref, wqk_ref, wv_ref, wo_ref, wz_ref, wg_ref, gb_ref, cw_ref, cb_ref,
                 qk_ref, v_ref, gate_ref, gates_ref, xb_ref, hb_ref, u_ref, *, tiles_per_seq):
    i = pl.program_id(0)
    j = pl.program_id(1)
    tm = x_ref.shape[0]
    halo = xh_ref.shape[0]

    @pl.when(j == 0)
    def _():
        xb = x_ref[...].astype(BF16)
        xb_ref[...] = xb
        keep = (i % tiles_per_seq != 0).astype(F32)
        hb_ref[...] = (xh_ref[...] * keep).astype(BF16)
        g = _dot_nt(xb, wg_ref[...].astype(BF16)) + gb_ref[...]
        lane = lax.broadcasted_iota(jnp.int32, g.shape, 1)
        log_f = jnp.minimum(g, 0.0) - jnp.log1p(jnp.exp(-jnp.abs(g)))
        gates_ref[...] = jnp.where(lane < NH_A, g, log_f)

    wqk = wqk_ref[...].astype(BF16)
    wv = wv_ref[...].astype(BF16)
    wo = wo_ref[...].astype(BF16)
    wz = wz_ref[...].astype(BF16)
    w = cw_ref[...]
    bias = cb_ref[...]
    scale = jnp.where(j >= pl.num_programs(1) // 2, DK_A ** -0.5, 1.0)
    u_ref[0:halo, :] = _dot_nt(hb_ref[...], wqk)
    for t in range(tm // ROW_SUB):
        rows = slice(t * ROW_SUB, (t + 1) * ROW_SUB)
        xs = xb_ref[rows, :]
        u_ref[halo + t * ROW_SUB:halo + (t + 1) * ROW_SUB, :] = _dot_nt(xs, wqk)
        v_ref[rows, :] = _dot_nt(xs, wv).astype(BF16)
        o = _dot_nt(xs, wo)
        z = _dot_nt(xs, wz)
        gate_ref[rows, :] = (jax.nn.sigmoid(o) * z * jax.nn.sigmoid(z)).astype(BF16)
        acc = jnp.broadcast_to(bias, (ROW_SUB, bias.shape[1]))
        for kk in range(CONV_A):
            off = halo - (CONV_A - 1) + kk + t * ROW_SUB
            acc = acc + w[kk:kk + 1, :] * u_ref[off:off + ROW_SUB, :]
        qk_ref[rows, :] = (acc * jax.nn.sigmoid(acc) * scale).astype(BF16)


def _a_in_proj(x2, w_in_t, gate_b, conv_w, conv_b, seq, tm=1024, tn=256):
    n = x2.shape[0]
    nb = D_MODEL // tn
    halo = 16
    n_main = 2 * QK_A + 3 * V_A
    wspec = lambda g: pl.BlockSpec((None, tn, D_MODEL), lambda i, j, g=g: (0, g * nb + j, 0))
    ospec = pl.BlockSpec((tm, tn), lambda i, j: (i, j))
    return pl.pallas_call(
        functools.partial(_a_in_kernel, tiles_per_seq=seq // tm),
        grid=(n // tm, nb),
        in_specs=[pl.BlockSpec((tm, D_MODEL), lambda i, j: (i, 0)),
                  pl.BlockSpec((halo, D_MODEL), lambda i, j: (jnp.maximum(i * (tm // halo) - 1, 0), 0)),
                  wspec(0), wspec(1), wspec(2), wspec(3),
                  pl.BlockSpec((None, 2 * NH_A, D_MODEL), lambda i, j: (0, n_main // (2 * NH_A), 0)),
                  pl.BlockSpec((1, 2 * NH_A), lambda i, j: (0, 0)),
                  pl.BlockSpec((CONV_A, tn), lambda i, j: (0, j)),
                  pl.BlockSpec((1, tn), lambda i, j: (0, j))],
        out_specs=[ospec, ospec, ospec, pl.BlockSpec((tm, 2 * NH_A), lambda i, j: (i, 0))],
        out_shape=[jax.ShapeDtypeStruct((n, 2 * QK_A), BF16),
                   jax.ShapeDtypeStruct((n, V_A), BF16),
                   jax.ShapeDtypeStruct((n, V_A), BF16),
                   jax.ShapeDtypeStruct((n, 2 * NH_A), F32)],
        scratch_shapes=[pltpu.VMEM((tm, D_MODEL), BF16),
                        pltpu.VMEM((halo, D_MODEL), BF16),
                        pltpu.VMEM((tm + halo, tn), F32)],
        compiler_params=pltpu.CompilerParams(
            dimension_semantics=("arbitrary", "arbitrary"), vmem_limit_bytes=VMEM_LIMIT),
        name="a_in_proj",
    )(x2, x2, w_in_t, w_in_t, w_in_t, w_in_t, w_in_t, gate_b, conv_w, conv_b)


def _mlstm_kernel(q_s, k_s, v_ref, gate_ref, gates_ref, hg_ref, out_ref, st_ref, m_ref, ig_s, bc_s, ar_s):
    seq = q_s.shape[0]
    L = MLSTM_CHUNK
    HP = MLSTM_HEADS
    nchunk = seq // L
    head0 = pl.program_id(1) * HP

    st_ref[...] = jnp.zeros(st_ref.shape, F32)
    m_ref[...] = jnp.zeros(m_ref.shape, F32)

    ng = 2 * NH_A
    sel_r = lax.broadcasted_iota(jnp.int32, (2 * ng, 2 * LANES), 0) % ng
    sel_c = lax.broadcasted_iota(jnp.int32, (2 * ng, 2 * LANES), 1)
    rr = lax.broadcasted_iota(jnp.int32, (L, L), 0)
    cc = lax.broadcasted_iota(jnp.int32, (L, L), 1)
    causal = cc <= rr
    tri = jnp.where(causal, 1.0, 0.0).astype(BF16)
    tri2 = jnp.concatenate([tri, tri], axis=1)
    ones_v = jnp.ones((L, LANES), BF16)
    heads = range(HP)
    qk_l = [slice(h * DK_A, (h + 1) * DK_A) for h in heads]
    v_l = [slice(h * DV_A, (h + 1) * DV_A) for h in heads]

    g_hi, g_lo = _split_bf16(gates_ref[...])
    g2 = jnp.concatenate([g_hi, g_lo], axis=1)
    for h in heads:
        want = jnp.where(sel_c < LANES, head0 + h, head0 + h + NH_A)
        onehot = jnp.where(sel_r == want, 1.0, 0.0).astype(BF16)
        iglf = _dot(g2, onehot)
        ig_s[h] = iglf[:, :LANES]
        lf_hi, lf_lo = _split_bf16(iglf[:, LANES:])
        for c in range(nchunk):
            rows = slice(c * L, (c + 1) * L)
            bc = _dot(tri2, jnp.concatenate([lf_hi[rows], lf_lo[rows]], axis=0))
            bc_s[h, rows, :] = bc
            ar_s[h, c * 8:(c + 1) * 8, :] = jnp.transpose(iglf[rows, :LANES] - bc)[0:8, :]

    def chunk(c, carry):
        r0 = pl.multiple_of(c * L, L)
        a0 = pl.multiple_of(c * 8, 8)
        qs = [q_s[pl.ds(r0, L), qk_l[h]] for h in heads]
        ks = [k_s[pl.ds(r0, L), qk_l[h]] for h in heads]
        k_ts = [jnp.transpose(ks[h]) for h in heads]
        vexts = [jnp.concatenate([v_ref[pl.ds(r0, L), v_l[h]], ones_v], axis=1) for h in heads]
        sts = [st_ref[h] for h in heads]
        m_sts = [m_ref[h] for h in heads]
        igs = [ig_s[h, pl.ds(r0, L), :] for h in heads]
        bcums = [bc_s[h, pl.ds(r0, L), :] for h in heads]
        a_rows = [ar_s[h, pl.ds(a0, 8), :][0:1, :] for h in heads]

        qk = [_dot_nt(qs[h], ks[h]) for h in heads]
        carrieds = [_dot(qs[h], sts[h].astype(BF16)) for h in heads]
        gtots = [bc[L - 1:L, :] for bc in bcums]
        dmats = [jnp.where(causal, bcums[h] + a_rows[h], NEG_BIG) for h in heads]
        inters = [bcums[h] + m_sts[h] for h in heads]
        m_qs = [jnp.maximum(inters[h], jnp.max(dmats[h], axis=1, keepdims=True)) for h in heads]
        scores = [(qk[h] * jnp.exp(dmats[h] - m_qs[h])).astype(BF16) for h in heads]
        intras = [_dot(scores[h], vexts[h]) for h in heads]

        for h in heads:
            wlog = gtots[h] - bcums[h] + igs[h]
            m_new = jnp.maximum(gtots[h] + m_sts[h], jnp.max(wlog, axis=0, keepdims=True))
            wk_row = jnp.exp(gtots[h] + a_rows[h] - m_new)
            kw_t = (k_ts[h].astype(F32) * wk_row).astype(BF16)
            decay = jnp.exp(gtots[h] + m_sts[h] - m_new)
            st_ref[h] = jnp.concatenate([decay] * 3, axis=1) * sts[h] + _dot(kw_t, vexts[h])
            m_ref[h] = m_new

        for h in heads:
            w_inter = jnp.exp(inters[h] - m_qs[h])
            tot = intras[h] + jnp.concatenate([w_inter] * 3, axis=1) * carrieds[h]
            inv = 1.0 / jnp.maximum(jnp.abs(tot[:, DV_A:]), jnp.exp(-m_qs[h]))
            hv = tot[:, :DV_A] * jnp.concatenate([inv, inv], axis=1)
            mu = jnp.mean(hv, axis=1, keepdims=True)
            hc = hv - mu
            var = jnp.mean(hc * hc, axis=1, keepdims=True)
            hn = hc * lax.rsqrt(var + LN_EPS)
            gate = gate_ref[pl.ds(r0, L), v_l[h]].astype(F32)
            out_ref[pl.ds(r0, L), v_l[h]] = (hn * hg_ref[:, v_l[h]] * gate).astype(BF16)
        return carry

    lax.fori_loop(0, nchunk, chunk, 0)


def _mlstm(qk, v, gate, gates, head_g, bsz, seq):
    n = bsz * seq
    hp = MLSTM_HEADS
    ngrp = NH_A // hp
    return pl.pallas_call(
        _mlstm_kernel,
        grid=(bsz, ngrp),
        in_specs=[pl.BlockSpec((seq, hp * DK_A), lambda b, h: (b, h)),
                  pl.BlockSpec((seq, hp * DK_A), lambda b, h: (b, ngrp + h)),
                  pl.BlockSpec((seq, hp * DV_A), lambda b, h: (b, h)),
                  pl.BlockSpec((seq, hp * DV_A), lambda b, h: (b, h)),
                  pl.BlockSpec((seq, 2 * NH_A), lambda b, h: (b, 0)),
                  pl.BlockSpec((1, hp * DV_A), lambda b, h: (0, h))],
        out_specs=pl.BlockSpec((seq, hp * DV_A), lambda b, h: (b, h)),
        out_shape=jax.ShapeDtypeStruct((n, V_A), BF16),
        scratch_shapes=[pltpu.VMEM((hp, DK_A, DV_A + LANES), F32),
                        pltpu.VMEM((hp, 1, LANES), F32),
                        pltpu.VMEM((hp, seq, LANES), F32),
                        pltpu.VMEM((hp, seq, LANES), F32),
                        pltpu.VMEM((hp, 8 * (seq // MLSTM_CHUNK), MLSTM_CHUNK), F32)],
        compiler_params=pltpu.CompilerParams(
            dimension_semantics=("arbitrary", "arbitrary"), vmem_limit_bytes=VMEM_LIMIT),
        name="mlstm",
    )(qk, qk, v, gate, gates, head_g)


def _out_ln_kernel(h_ref, x_ref, w_ref, g_ref, b_ref, *out_refs):
    tm, d = x_ref.shape
    inv_d = 1.0 / d
    for t in range(tm // ROW_SUB):
        rows = slice(t * ROW_SUB, (t + 1) * ROW_SUB)
        hs = h_ref[rows, :]
        s1 = jnp.zeros((ROW_SUB, LANES), F32)
        s2 = jnp.zeros((ROW_SUB, LANES), F32)
        for c in range(d // LN_COL_CHUNK):
            cols = slice(c * LN_COL_CHUNK, (c + 1) * LN_COL_CHUNK)
            r = ALPHA * x_ref[rows, cols] + _dot(hs, w_ref[:, cols])
            out_refs[0][rows, cols] = r
            for l in range(LN_COL_CHUNK // LANES):
                rl = r[:, l * LANES:(l + 1) * LANES]
                s1 = s1 + rl
                s2 = s2 + rl * rl
        mu = jnp.sum(s1, axis=1, keepdims=True) * inv_d
        var = jnp.sum(s2, axis=1, keepdims=True) * inv_d - mu * mu
        rstd = lax.rsqrt(var + LN_EPS)
        for c in range(d // LN_COL_CHUNK):
            cols = slice(c * LN_COL_CHUNK, (c + 1) * LN_COL_CHUNK)
            out = (out_refs[0][rows, cols] - mu) * rstd * g_ref[:, cols] + b_ref[:, cols]
            out_refs[0][rows, cols] = out
            if len(out_refs) > 1:
                out_refs[1][rows, cols] = out.astype(BF16)


def _out_ln(h, x2, w, ln_g, ln_b, with_bf16, tm=512):
    n = x2.shape[0]
    row = pl.BlockSpec((tm, D_MODEL), lambda i: (i, 0))
    vec = pl.BlockSpec((1, D_MODEL), lambda i: (0, 0))
    out_shape = [jax.ShapeDtypeStruct((n, D_MODEL), F32)]
    out_specs = [row]
    if with_bf16:
        out_shape.append(jax.ShapeDtypeStruct((n, D_MODEL), BF16))
        out_specs.append(row)
    return pl.pallas_call(
        _out_ln_kernel,
        grid=(n // tm,),
        in_specs=[row, row, pl.BlockSpec((D_MODEL, D_MODEL), lambda i: (0, 0)), vec, vec],
        out_specs=out_specs,
        out_shape=out_shape,
        compiler_params=pltpu.CompilerParams(
            dimension_semantics=("arbitrary",), vmem_limit_bytes=VMEM_LIMIT),
        name="out_ln",
    )(h, x2, w, ln_g, ln_b)


def _b_proj_kernel(x_ref, wk_ref, wv_ref, wq_ref, wz_ref, k_ref, v_ref, q_ref, zg_ref):
    wk = wk_ref[...].astype(BF16)
    wv = wv_ref[...].astype(BF16)
    wq = wq_ref[...].astype(BF16)
    wz = wz_ref[...].astype(BF16)
    for t in range(x_ref.shape[0] // ROW_SUB):
        rows = slice(t * ROW_SUB, (t + 1) * ROW_SUB)
        xs = x_ref[rows, :]
        k_ref[rows, :] = _dot(xs, wk).astype(BF16)
        v_ref[rows, :] = _dot(xs, wv).astype(BF16)
        q_ref[rows, :] = (_dot(xs, wq) * (DH_B ** -0.5)).astype(BF16)
        z = _dot(xs, wz)
        zg_ref[rows, :] = (z * jax.nn.sigmoid(z)).astype(BF16)


def _b_proj(x1b, kv_w, b_w_in, tm=1024, tn=256):
    n = x1b.shape[0]
    nb = W_B // tn
    wspec = lambda g: pl.BlockSpec((D_MODEL, tn), lambda i, j, g=g: (0, g * nb + j))
    wspec3 = lambda g: pl.BlockSpec((None, D_MODEL, tn), lambda i, j, g=g: (0, 0, g * nb + j))
    ospec = pl.BlockSpec((tm, tn), lambda i, j: (i, j))
    sds = jax.ShapeDtypeStruct((n, W_B), BF16)
    return pl.pallas_call(
        _b_proj_kernel,
        grid=(n // tm, nb),
        in_specs=[pl.BlockSpec((tm, D_MODEL), lambda i, j: (i, 0)), wspec(0), wspec(1), wspec3(0), wspec3(1)],
        out_specs=[ospec] * 4,
        out_shape=[sds] * 4,
        compiler_params=pltpu.CompilerParams(
            dimension_semantics=("arbitrary", "arbitrary"), vmem_limit_bytes=VMEM_LIMIT),
        name="b_proj",
    )(x1b, kv_w, kv_w, b_w_in, b_w_in)


def _sb_kernel(q_ref, k_ref, v_ref, zg_ref, o_ref):
    seq = q_ref.shape[0]
    T = SB_TILE
    HP = q_ref.shape[1] // DH_B
    rr = lax.broadcasted_iota(jnp.int32, (T, T), 0)
    cc = lax.broadcasted_iota(jnp.int32, (T, T), 1)
    below = cc < rr
    neg_suffix = jnp.where(rr > cc, -1.0, 0.0).astype(BF16)

    def tiles(q0, k0, carries, accs, masked):
        lanes = [slice(hh * DH_B, (hh + 1) * DH_B) for hh in range(HP)]
        zs = [_dot_nt(q_ref[pl.ds(q0, T), ln], k_ref[pl.ds(k0, T), ln]) for ln in lanes]
        sps, log_betas = [], []
        for z in zs:
            sp = jnp.maximum(z, 0.0) + jnp.log(1.0 + jnp.exp(-jnp.abs(z)))
            log_betas.append(z - sp)
            sps.append(jnp.where(below, sp, 0.0) if masked else sp)
        betweens = [_dot(sp.astype(BF16), neg_suffix) for sp in sps]
        probs = []
        for log_beta, between, carry in zip(log_betas, betweens, carries):
            a = jnp.exp(log_beta + between + carry)
            probs.append((jnp.where(below, a, 0.0) if masked else a).astype(BF16))
        accs = tuple(acc + _dot(a, v_ref[pl.ds(k0, T), ln]) for acc, a, ln in zip(accs, probs, lanes))
        carries = tuple(carry - jnp.sum(sp, axis=1, keepdims=True) for carry, sp in zip(carries, sps))
        return carries, accs

    def any_alive(carries):
        m = carries[0]
        for c in carries[1:]:
            m = jnp.maximum(m, c)
        return (jnp.max(m) > SB_DEAD_LOG).astype(jnp.int32)

    def qblock(qi, c):
        q0 = pl.multiple_of(qi * T, T)
        zc = (jnp.zeros((T, 1), F32),) * HP
        za = (jnp.zeros((T, DH_B), F32),) * HP
        carries, accs = tiles(q0, q0, zc, za, True)

        def cond(s):
            return jnp.logical_and(s[0] >= 0, s[1] > 0)

        def body(s):
            kb, _, carries, accs = s
            carries, accs = tiles(q0, pl.multiple_of(kb * T, T), carries, accs, False)
            return kb - 1, any_alive(carries), carries, accs

        _, _, _, accs = lax.while_loop(cond, body, (qi - 1, any_alive(carries), carries, accs))
        for hh in range(HP):
            lanes = slice(hh * DH_B, (hh + 1) * DH_B)
            o_ref[pl.ds(q0, T), lanes] = (accs[hh] * zg_ref[pl.ds(q0, T), lanes].astype(F32)).astype(BF16)
        return c

    lax.fori_loop(0, seq // T, qblock, 0)


def _stickbreak(q, k, v, zg, bsz, seq):
    n = bsz * seq
    spec = pl.BlockSpec((seq, SB_HEADS * DH_B), lambda b, h: (b, h))
    return pl.pallas_call(
        _sb_kernel,
        grid=(bsz, NH_B // SB_HEADS),
        in_specs=[spec] * 4,
        out_specs=spec,
        out_shape=jax.ShapeDtypeStruct((n, W_B), BF16),
        compiler_params=pltpu.CompilerParams(
            dimension_semantics=("arbitrary", "arbitrary"), vmem_limit_bytes=VMEM_LIMIT),
        name="stickbreak",
    )(q, k, v, zg)


def kernel(x, a_w_in, a_gate_b, a_conv_w, a_conv_b, a_head_g, a_w_out, a_ln_g, a_ln_b,
           kv_w, b_w_in, b_w_out, b_ln_g, b_ln_b):
    bsz, seq, d = x.shape
    assert d == D_MODEL and a_w_in.shape[0] == 1 and b_w_in.shape[0] == 1
    n = bsz * seq
    x2 = x.reshape(n, d)

    qk, v, gate, gates = _a_in_proj(x2, jnp.swapaxes(a_w_in, 1, 2), a_gate_b[0].reshape(1, -1),
                                    a_conv_w[0], a_conv_b[0].reshape(1, -1), seq)
    hg = _mlstm(qk, v, gate, gates, a_head_g[0].reshape(1, -1), bsz, seq)
    x1, x1b = _out_ln(hg, x2, a_w_out[0].astype(BF16), a_ln_g[0].reshape(1, -1), a_ln_b[0].reshape(1, -1), True)

    k_sh, v_sh, q, zg = _b_proj(x1b, kv_w, b_w_in)
    att = _stickbreak(q, k_sh, v_sh, zg, bsz, seq)
    (out,) = _out_ln(att, x1, b_w_out[0].astype(BF16), b_ln_g[0].reshape(1, -1), b_ln_b[0].reshape(1, -1), False)
    return out.reshape(bsz, seq, d)
```

```python
import functools

import jax
import jax.numpy as jnp
from jax import lax
from jax.experimental import pallas as pl
from jax.experimental.pallas import tpu as pltpu

F32 = jnp.float32
BF16 = jnp.bfloat16

D_MODEL = 2048
NH_A = 8
DK_A = 128
DV_A = 256
QK_A = NH_A * DK_A
V_A = NH_A * DV_A
CONV_A = 4
NH_B = 16
DH_B = 128
W_B = NH_B * DH_B
DEPTH = 2
ALPHA = (2.0 * DEPTH) ** 0.25
LN_EPS = 1e-5

LANES = 128
VMEM_LIMIT = 56 * 1024 * 1024

MLSTM_CHUNK = 128
MLSTM_HEADS = 4
ROW_SUB = 256
LN_COL_CHUNK = 512
SB_TILE = 256
SB_HEADS = 4
SB_DEAD_LOG = -104.0
NEG_BIG = -1e30


def _dot(a, b):
    return jnp.dot(a, b, preferred_element_type=F32)


def _dot_nt(a, b):
    return lax.dot_general(a, b, (((1,), (1,)), ((), ())), preferred_element_type=F32)


def _dot_tn(a, b):
    return lax.dot_general(a, b, (((0,), (0,)), ((), ())), preferred_element_type=F32)


def _split_bf16(x):
    hi = x.astype(BF16)
    lo = (x - hi.astype(F32)).astype(BF16)
    return hi, lo


def _a_in_kernel(x_ref, xh_ref, wqk_ref, wv_ref, wo_ref, wz_ref, wg_ref, gb_ref, cw_ref, cb_ref,
                 qk_ref, v_ref, gate_ref, gates_ref, xb_ref, hb_ref, u_ref, *, tiles_per_seq):
    i = pl.program_id(0)
    j = pl.program_id(1)
    tm = x_ref.shape[0]
    halo = xh_ref.shape[0]

    @pl.when(j == 0)
    def _():
        xb = x_ref[...].astype(BF16)
        xb_ref[...] = xb
        keep = (i % tiles_per_seq != 0).astype(F32)
        hb_ref[...] = (xh_ref[...] * keep).astype(BF16)
        g = _dot_nt(xb, wg_ref[...].astype(BF16)) + gb_ref[...]
        lane = lax.broadcasted_iota(jnp.int32, g.shape, 1)
        log_f = jnp.minimum(g, 0.0) - jnp.log1p(jnp.exp(-jnp.abs(g)))
        gates_ref[...] = jnp.where(lane < NH_A, g, log_f)

    wqk = wqk_ref[...].astype(BF16)
    wv = wv_ref[...].astype(BF16)
    wo = wo_ref[...].astype(BF16)
    wz = wz_ref[...].astype(BF16)
    w = cw_ref[...]
    bias = cb_ref[...]
    scale = jnp.where(j >= pl.num_programs(1) // 2, DK_A ** -0.5, 1.0)
    u_ref[0:halo, :] = _dot_nt(hb_ref[...], wqk)
    for t in range(tm // ROW_SUB):
        rows = slice(t * ROW_SUB, (t + 1) * ROW_SUB)
        xs = xb_ref[rows, :]
        u_ref[halo + t * ROW_SUB:halo + (t + 1) * ROW_SUB, :] = _dot_nt(xs, wqk)
        o = _dot_nt(xs, wo)
        z = _dot_nt(xs, wz)
        gate_ref[rows, :] = (jax.nn.sigmoid(o) * z * jax.nn.sigmoid(z)).astype(BF16)
        acc = jnp.broadcast_to(bias, (ROW_SUB, bias.shape[1]))
        for kk in range(CONV_A):
            off = halo - (CONV_A - 1) + kk + t * ROW_SUB
            acc = acc + w[kk:kk + 1, :] * u_ref[off:off + ROW_SUB, :]
        qk_ref[rows, :] = (acc * jax.nn.sigmoid(acc) * scale).astype(BF16)
        v_ref[rows, :] = _dot_nt(xs, wv).astype(BF16)


def _a_in_proj(x2, w_in_t, gate_b, conv_w, conv_b, seq, tm=1024, tn=256):
    n = x2.shape[0]
    nb = D_MODEL // tn
    halo = 16
    n_main = 2 * QK_A + 3 * V_A
    wspec = lambda g: pl.BlockSpec((None, tn, D_MODEL), lambda i, j, g=g: (0, g * nb + j, 0))
    ospec = pl.BlockSpec((tm, tn), lambda i, j: (i, j))
    return pl.pallas_call(
        functools.partial(_a_in_kernel, tiles_per_seq=seq // tm),
        grid=(n // tm, nb),
        in_specs=[pl.BlockSpec((tm, D_MODEL), lambda i, j: (i, 0)),
                  pl.BlockSpec((halo, D_MODEL), lambda i, j: (jnp.maximum(i * (tm // halo) - 1, 0), 0)),
                  wspec(0), wspec(1), wspec(2), wspec(3),
                  pl.BlockSpec((None, 2 * NH_A, D_MODEL), lambda i, j: (0, n_main // (2 * NH_A), 0)),
                  pl.BlockSpec((1, 2 * NH_A), lambda i, j: (0, 0)),
                  pl.BlockSpec((CONV_A, tn), lambda i, j: (0, j)),
                  pl.BlockSpec((1, tn), lambda i, j: (0, j))],
        out_specs=[ospec, ospec, ospec, pl.BlockSpec((tm, 2 * NH_A), lambda i, j: (i, 0))],
        out_shape=[jax.ShapeDtypeStruct((n, 2 * QK_A), BF16),
                   jax.ShapeDtypeStruct((n, V_A), BF16),
                   jax.ShapeDtypeStruct((n, V_A), BF16),
                   jax.ShapeDtypeStruct((n, 2 * NH_A), F32)],
        scratch_shapes=[pltpu.VMEM((tm, D_MODEL), BF16),
                        pltpu.VMEM((halo, D_MODEL), BF16),
                        pltpu.VMEM((tm + halo, tn), F32)],
        compiler_params=pltpu.CompilerParams(
            dimension_semantics=("arbitrary", "arbitrary"), vmem_limit_bytes=VMEM_LIMIT),
        name="a_in_proj",
    )(x2, x2, w_in_t, w_in_t, w_in_t, w_in_t, w_in_t, gate_b, conv_w, conv_b)


def _mlstm_kernel(q_s, k_s, v_ref, gate_ref, gates_ref, hg_ref, out_ref, st_ref, m_ref, ig_s, bc_s, ar_s):
    seq = q_s.shape[0]
    L = MLSTM_CHUNK
    HP = MLSTM_HEADS
    nchunk = seq // L
    head0 = pl.program_id(1) * HP

    st_ref[...] = jnp.zeros(st_ref.shape, F32)
    m_ref[...] = jnp.zeros(m_ref.shape, F32)

    ng = 2 * NH_A
    sel_r = lax.broadcasted_iota(jnp.int32, (2 * ng, 2 * LANES), 0) % ng
    sel_c = lax.broadcasted_iota(jnp.int32, (2 * ng, 2 * LANES), 1)
    rr = lax.broadcasted_iota(jnp.int32, (L, L), 0)
    cc = lax.broadcasted_iota(jnp.int32, (L, L), 1)
    causal = cc <= rr
    tri = jnp.where(causal, 1.0, 0.0).astype(BF16)
    tri2 = jnp.concatenate([tri, tri], axis=1)
    ones_v = jnp.ones((L, LANES), BF16)
    heads = range(HP)
    qk_l = [slice(h * DK_A, (h + 1) * DK_A) for h in heads]
    v_l = [slice(h * DV_A, (h + 1) * DV_A) for h in heads]

    g_hi, g_lo = _split_bf16(gates_ref[...])
    g2 = jnp.concatenate([g_hi, g_lo], axis=1)
    for h in heads:
        want = jnp.where(sel_c < LANES, head0 + h, head0 + h + NH_A)
        onehot = jnp.where(sel_r == want, 1.0, 0.0).astype(BF16)
        iglf = _dot(g2, onehot)
        ig_s[h] = iglf[:, :LANES]
        lf_hi, lf_lo = _split_bf16(iglf[:, LANES:])
        for c in range(nchunk):
            rows = slice(c * L, (c + 1) * L)
            bc = _dot(tri2, jnp.concatenate([lf_hi[rows], lf_lo[rows]], axis=0))
            bc_s[h, rows, :] = bc
            ar_s[h, c * 8:(c + 1) * 8, :] = jnp.transpose(iglf[rows, :LANES] - bc)[0:8, :]

    def chunk(c, carry):
        r0 = pl.multiple_of(c * L, L)
        a0 = pl.multiple_of(c * 8, 8)
        qs = [q_s[pl.ds(r0, L), qk_l[h]] for h in heads]
        ks = [k_s[pl.ds(r0, L), qk_l[h]] for h in heads]
        k_ts = [jnp.transpose(ks[h]) for h in heads]
        vexts = [jnp.concatenate([v_ref[pl.ds(r0, L), v_l[h]], ones_v], axis=1) for h in heads]
        sts = [st_ref[h] for h in heads]
        m_sts = [m_ref[h] for h in heads]
        igs = [ig_s[h, pl.ds(r0, L), :] for h in heads]
        bcums = [bc_s[h, pl.ds(r0, L), :] for h in heads]
        a_rows = [ar_s[h, pl.ds(a0, 8), :][0:1, :] for h in heads]

        qk = [_dot_nt(qs[h], ks[h]) for h in heads]
        carrieds = [_dot(qs[h], sts[h].astype(BF16)) for h in heads]
        gtots = [bc[L - 1:L, :] for bc in bcums]
        dmats = [jnp.where(causal, bcums[h] + a_rows[h], NEG_BIG) for h in heads]
        inters = [bcums[h] + m_sts[h] for h in heads]
        m_qs = [jnp.maximum(inters[h], jnp.max(dmats[h], axis=1, keepdims=True)) for h in heads]
        scores = [(qk[h] * jnp.exp(dmats[h] - m_qs[h])).astype(BF16) for h in heads]
        intras = [_dot(scores[h], vexts[h]) for h in heads]

        for h in heads:
            wlog = gtots[h] - bcums[h] + igs[h]
            m_new = jnp.maximum(gtots[h] + m_sts[h], jnp.max(wlog, axis=0, keepdims=True))
            wk_row = jnp.exp(gtots[h] + a_rows[h] - m_new)
            kw_t = (k_ts[h].astype(F32) * wk_row).astype(BF16)
            decay = jnp.exp(gtots[h] + m_sts[h] - m_new)
            st_ref[h] = jnp.concatenate([decay] * 3, axis=1) * sts[h] + _dot(kw_t, vexts[h])
            m_ref[h] = m_new

        for h in heads:
            w_inter = jnp.exp(inters[h] - m_qs[h])
            tot = intras[h] + jnp.concatenate([w_inter] * 3, axis=1) * carrieds[h]
            inv = 1.0 / jnp.maximum(jnp.abs(tot[:, DV_A:]), jnp.exp(-m_qs[h]))
            hv = tot[:, :DV_A] * jnp.concatenate([inv, inv], axis=1)
            mu = jnp.mean(hv, axis=1, keepdims=True)
            hc = hv - mu
            var = jnp.mean(hc * hc, axis=1, keepdims=True)
            hn = hc * lax.rsqrt(var + LN_EPS)
            gate = gate_ref[pl.ds(r0, L), v_l[h]].astype(F32)
            out_ref[pl.ds(r0, L), v_l[h]] = (hn * hg_ref[:, v_l[h]] * gate).astype(BF16)
        return carry

    lax.fori_loop(0, nchunk, chunk, 0)


def _mlstm(qk, v, gate, gates, head_g, bsz, seq):
    n = bsz * seq
    hp = MLSTM_HEADS
    ngrp = NH_A // hp
    return pl.pallas_call(
        _mlstm_kernel,
        grid=(bsz, ngrp),
        in_specs=[pl.BlockSpec((seq, hp * DK_A), lambda b, h: (b, h)),
                  pl.BlockSpec((seq, hp * DK_A), lambda b, h: (b, ngrp + h)),
                  pl.BlockSpec((seq, hp * DV_A), lambda b, h: (b, h)),
                  pl.BlockSpec((seq, hp * DV_A), lambda b, h: (b, h)),
                  pl.BlockSpec((seq, 2 * NH_A), lambda b, h: (b, 0)),
                  pl.BlockSpec((1, hp * DV_A), lambda b, h: (0, h))],
        out_specs=pl.BlockSpec((seq, hp * DV_A), lambda b, h: (b, h)),
        out_shape=jax.ShapeDtypeStruct((n, V_A), BF16),
        scratch_shapes=[pltpu.VMEM((hp, DK_A, DV_A + LANES), F32),
                        pltpu.VMEM((hp, 1, LANES), F32),
                        pltpu.VMEM((hp, seq, LANES), F32),
                        pltpu.VMEM((hp, seq, LANES), F32),
                        pltpu.VMEM((hp, 8 * (seq // MLSTM_CHUNK), MLSTM_CHUNK), F32)],
        compiler_params=pltpu.CompilerParams(
            dimension_semantics=("arbitrary", "arbitrary"), vmem_limit_bytes=VMEM_LIMIT),
        name="mlstm",
    )(qk, qk, v, gate, gates, head_g)


def _out_ln_kernel(h_ref, x_ref, w_ref, g_ref, b_ref, *out_refs):
    tm, d = x_ref.shape
    inv_d = 1.0 / d
    for t in range(tm // ROW_SUB):
        rows = slice(t * ROW_SUB, (t + 1) * ROW_SUB)
        hs = h_ref[rows, :]
        s1 = jnp.zeros((ROW_SUB, LANES), F32)
        s2 = jnp.zeros((ROW_SUB, LANES), F32)
        for c in range(d // LN_COL_CHUNK):
            cols = slice(c * LN_COL_CHUNK, (c + 1) * LN_COL_CHUNK)
            r = ALPHA * x_ref[rows, cols] + _dot(hs, w_ref[:, cols])
            out_refs[0][rows, cols] = r
            for l in range(LN_COL_CHUNK // LANES):
                rl = r[:, l * LANES:(l + 1) * LANES]
                s1 = s1 + rl
                s2 = s2 + rl * rl
        mu = jnp.sum(s1, axis=1, keepdims=True) * inv_d
        var = jnp.sum(s2, axis=1, keepdims=True) * inv_d - mu * mu
        rstd = lax.rsqrt(var + LN_EPS)
        for c in range(d // LN_COL_CHUNK):
            cols = slice(c * LN_COL_CHUNK, (c + 1) * LN_COL_CHUNK)
            out = (out_refs[0][rows, cols] - mu) * rstd * g_ref[:, cols] + b_ref[:, cols]
            out_refs[0][rows, cols] = out
            if len(out_refs) > 1:
                out_refs[1][rows, cols] = out.astype(BF16)


def _out_ln(h, x2, w, ln_g, ln_b, with_bf16, tm=512):
    n = x2.shape[0]
    row = pl.BlockSpec((tm, D_MODEL), lambda i: (i, 0))
    vec = pl.BlockSpec((1, D_MODEL), lambda i: (0, 0))
    out_shape = [jax.ShapeDtypeStruct((n, D_MODEL), F32)]
    out_specs = [row]
    if with_bf16:
        out_shape.append(jax.ShapeDtypeStruct((n, D_MODEL), BF16))
        out_specs.append(row)
    return pl.pallas_call(
        _out_ln_kernel,
        grid=(n // tm,),
        in_specs=[row, row, pl.BlockSpec((D_MODEL, D_MODEL), lambda i: (0, 0)), vec, vec],
        out_specs=out_specs,
        out_shape=out_shape,
        compiler_params=pltpu.CompilerParams(
            dimension_semantics=("arbitrary",), vmem_limit_bytes=VMEM_LIMIT),
        name="out_ln",
    )(h, x2, w, ln_g, ln_b)


def _b_proj_kernel(x_ref, wk_ref, wv_ref, wq_ref, wz_ref, k_ref, v_ref, q_ref, zg_ref):
    wk = wk_ref[...].astype(BF16)
    wv = wv_ref[...].astype(BF16)
    wq = wq_ref[...].astype(BF16)
    wz = wz_ref[...].astype(BF16)
    for t in range(x_ref.shape[0] // ROW_SUB):
        rows = slice(t * ROW_SUB, (t + 1) * ROW_SUB)
        xs = x_ref[rows, :]
        k_ref[rows, :] = _dot(xs, wk).astype(BF16)
        v_ref[rows, :] = _dot(xs, wv).astype(BF16)
        q_ref[rows, :] = (_dot(xs, wq) * (DH_B ** -0.5)).astype(BF16)
        z = _dot(xs, wz)
        zg_ref[rows, :] = (z * jax.nn.sigmoid(z)).astype(BF16)


def _b_proj(x1b, kv_w, b_w_in, tm=1024, tn=256):
    n = x1b.shape[0]
    nb = W_B // tn
    wspec = lambda g: pl.BlockSpec((D_MODEL, tn), lambda i, j, g=g: (0, g * nb + j))
    wspec3 = lambda g: pl.BlockSpec((None, D_MODEL, tn), lambda i, j, g=g: (0, 0, g * nb + j))
    ospec = pl.BlockSpec((tm, tn), lambda i, j: (i, j))
    sds = jax.ShapeDtypeStruct((n, W_B), BF16)
    return pl.pallas_call(
        _b_proj_kernel,
        grid=(n // tm, nb),
        in_specs=[pl.BlockSpec((tm, D_MODEL), lambda i, j: (i, 0)), wspec(0), wspec(1), wspec3(0), wspec3(1)],
        out_specs=[ospec] * 4,
        out_shape=[sds] * 4,
        compiler_params=pltpu.CompilerParams(
            dimension_semantics=("arbitrary", "arbitrary"), vmem_limit_bytes=VMEM_LIMIT),
        name="b_proj",
    )(x1b, kv_w, kv_w, b_w_in, b_w_in)


def _sb_kernel(q_ref, k_ref, v_ref, zg_ref, o_ref, acc_ref, carry_ref):
    seq = q_ref.shape[0]
    T = SB_TILE
    HP = q_ref.shape[1] // DH_B
    rr = lax.broadcasted_iota(jnp.int32, (T, T), 0)
    cc = lax.broadcasted_iota(jnp.int32, (T, T), 1)
    below = cc < rr
    neg_suffix = jnp.where(rr > cc, -1.0, 0.0).astype(BF16)
    lanes = [slice(hh * DH_B, (hh + 1) * DH_B) for hh in range(HP)]

    def tiles(q0, k0, diagonal):
        zs = [_dot_nt(q_ref[pl.ds(q0, T), ln], k_ref[pl.ds(k0, T), ln]) for ln in lanes]
        sps, log_betas = [], []
        for z in zs:
            sp = jnp.maximum(z, 0.0) + jnp.log(1.0 + jnp.exp(-jnp.abs(z)))
            log_betas.append(z - sp)
            sps.append(jnp.where(below, sp, 0.0) if diagonal else sp)
        betweens = [_dot(sp.astype(BF16), neg_suffix) for sp in sps]
        alive = None
        for hh in range(HP):
            shift = betweens[hh] if diagonal else betweens[hh] + carry_ref[hh]
            a = jnp.exp(log_betas[hh] + shift)
            if diagonal:
                a = jnp.where(below, a, 0.0)
            av = _dot(a.astype(BF16), v_ref[pl.ds(k0, T), lanes[hh]])
            acc_ref[hh] = av if diagonal else acc_ref[hh] + av
            total = betweens[hh][:, 0:1] - sps[hh][:, 0:1]
            carry = total if diagonal else carry_ref[hh] + total
            carry_ref[hh] = carry
            alive = carry if alive is None else jnp.maximum(alive, carry)
        return (jnp.max(alive) > SB_DEAD_LOG).astype(jnp.int32)

    def qblock(qi, c):
        q0 = pl.multiple_of(qi * T, T)
        alive = tiles(q0, q0, True)

        def cond(s):
            return jnp.logical_and(s[0] >= 0, s[1] > 0)

        def body(s):
            return s[0] - 1, tiles(q0, pl.multiple_of(s[0] * T, T), False)

        lax.while_loop(cond, body, (qi - 1, alive))
        for hh in range(HP):
            o_ref[pl.ds(q0, T), lanes[hh]] = (acc_ref[hh] * zg_ref[pl.ds(q0, T), lanes[hh]].astype(F32)).astype(BF16)
        return c

    lax.fori_loop(0, seq // T, qblock, 0)


def _stickbreak(q, k, v, zg, bsz, seq):
    n = bsz * seq
    spec = pl.BlockSpec((seq, SB_HEADS * DH_B), lambda b, h: (b, h))
    return pl.pallas_call(
        _sb_kernel,
        grid=(bsz, NH_B // SB_HEADS),
        in_specs=[spec] * 4,
        out_specs=spec,
        out_shape=jax.ShapeDtypeStruct((n, W_B), BF16),
        scratch_shapes=[pltpu.VMEM((SB_HEADS, SB_TILE, DH_B), F32),
                        pltpu.VMEM((SB_HEADS, SB_TILE, 1), F32)],
        compiler_params=pltpu.CompilerParams(
            dimension_semantics=("arbitrary", "arbitrary"), vmem_limit_bytes=VMEM_LIMIT),
        name="stickbreak",
    )(q, k, v, zg)


def kernel(x, a_w_in, a_gate_b, a_conv_w, a_conv_b, a_head_g, a_w_out, a_ln_g, a_ln_b,
           kv_w, b_w_in, b_w_out, b_ln_g, b_ln_b):
    bsz, seq, d = x.shape
    assert d == D_MODEL and a_w_in.shape[0] == 1 and b_w_in.shape[0] == 1
    n = bsz * seq
    x2 = x.reshape(n, d)

    qk, v, gate, gates = _a_in_proj(x2, jnp.swapaxes(a_w_in, 1, 2), a_gate_b[0].reshape(1, -1),
                                    a_conv_w[0], a_conv_b[0].reshape(1, -1), seq)
    hg = _mlstm(qk, v, gate, gates, a_head_g[0].reshape(1, -1), bsz, seq)
    x1, x1b = _out_ln(hg, x2, a_w_out[0].astype(BF16), a_ln_g[0].reshape(1, -1), a_ln_b[0].reshape(1, -1), True)

    k_sh, v_sh, q, zg = _b_proj(x1b, kv_w, b_w_in)
    att = _stickbreak(q, k_sh, v_sh, zg, bsz, seq)
    (out,) = _out_ln(att, x1, b_w_out[0].astype(BF16), b_ln_g[0].reshape(1, -1), b_ln_b[0].reshape(1, -1), False)
    return out.reshape(bsz, seq, d)
```

```python
import functools

import jax
import jax.numpy as jnp
from jax import lax
from jax.experimental import pallas as pl
from jax.experimental.pallas import tpu as pltpu

F32 = jnp.float32
BF16 = jnp.bfloat16

D_MODEL = 2048
NH_A = 8
DK_A = 128
DV_A = 256
QK_A = NH_A * DK_A
V_A = NH_A * DV_A
CONV_A = 4
NH_B = 16
DH_B = 128
W_B = NH_B * DH_B
DEPTH = 2
ALPHA = (2.0 * DEPTH) ** 0.25
LN_EPS = 1e-5

LANES = 128
VMEM_LIMIT = 56 * 1024 * 1024

MLSTM_CHUNK = 128
MLSTM_HEADS = 4
ROW_SUB = 256
LN_COL_CHUNK = 512
SB_TILE = 256
SB_HEADS = 4
SB_DEAD_LOG = -104.0
NEG_BIG = -1e30


def _dot(a, b):
    return jnp.dot(a, b, preferred_element_type=F32)


def _dot_nt(a, b):
    return lax.dot_general(a, b, (((1,), (1,)), ((), ())), preferred_element_type=F32)


def _dot_tn(a, b):
    return lax.dot_general(a, b, (((0,), (0,)), ((), ())), preferred_element_type=F32)


def _split_bf16(x):
    hi = x.astype(BF16)
    lo = (x - hi.astype(F32)).astype(BF16)
    return hi, lo


def _a_in_kernel(x_ref, xh_ref, wqk_ref, wv_ref, wo_ref, wz_ref, wg_ref, gb_ref, cw_ref, cb_ref,
                 qk_ref, v_ref, gate_ref, gates_ref, xb_ref, hb_ref, u_ref, *, tiles_per_seq):
    i = pl.program_id(0)
    j = pl.program_id(1)
    tm = x_ref.shape[0]
    halo = xh_ref.shape[0]

    @pl.when(j == 0)
    def _():
        xb = x_ref[...].astype(BF16)
        xb_ref[...] = xb
        keep = (i % tiles_per_seq != 0).astype(F32)
        hb_ref[...] = (xh_ref[...] * keep).astype(BF16)
        g = _dot_nt(xb, wg_ref[...].astype(BF16)) + gb_ref[...]
        lane = lax.broadcasted_iota(jnp.int32, g.shape, 1)
        log_f = jnp.minimum(g, 0.0) - jnp.log1p(jnp.exp(-jnp.abs(g)))
        gates_ref[...] = jnp.where(lane < NH_A, g, log_f)

    wqk = wqk_ref[...].astype(BF16)
    wv = wv_ref[...].astype(BF16)
    wo = wo_ref[...].astype(BF16)
    wz = wz_ref[...].astype(BF16)
    w = cw_ref[...]
    bias = cb_ref[...]
    scale = jnp.where(j >= pl.num_programs(1) // 2, DK_A ** -0.5, 1.0)
    u_ref[0:halo, :] = _dot_nt(hb_ref[...], wqk)
    for t in range(tm // ROW_SUB):
        rows = slice(t * ROW_SUB, (t + 1) * ROW_SUB)
        xs = xb_ref[rows, :]
        u_ref[halo + t * ROW_SUB:halo + (t + 1) * ROW_SUB, :] = _dot_nt(xs, wqk)
        o = _dot_nt(xs, wo)
        z = _dot_nt(xs, wz)
        gate_ref[rows, :] = (jax.nn.sigmoid(o) * z * jax.nn.sigmoid(z)).astype(BF16)
        acc = jnp.broadcast_to(bias, (ROW_SUB, bias.shape[1]))
        for kk in range(CONV_A):
            off = halo - (CONV_A - 1) + kk + t * ROW_SUB
            acc = acc + w[kk:kk + 1, :] * u_ref[off:off + ROW_SUB, :]
        qk_ref[rows, :] = (acc * jax.nn.sigmoid(acc) * scale).astype(BF16)
        v_ref[rows, :] = _dot_nt(xs, wv).astype(BF16)


def _a_in_proj(x2, w_in_t, gate_b, conv_w, conv_b, seq, tm=1024, tn=256):
    n = x2.shape[0]
    nb = D_MODEL // tn
    halo = 16
    n_main = 2 * QK_A + 3 * V_A
    wspec = lambda g: pl.BlockSpec((None, tn, D_MODEL), lambda i, j, g=g: (0, g * nb + j, 0))
    ospec = pl.BlockSpec((tm, tn), lambda i, j: (i, j))
    return pl.pallas_call(
        functools.partial(_a_in_kernel, tiles_per_seq=seq // tm),
        grid=(n // tm, nb),
        in_specs=[pl.BlockSpec((tm, D_MODEL), lambda i, j: (i, 0)),
                  pl.BlockSpec((halo, D_MODEL), lambda i, j: (jnp.maximum(i * (tm // halo) - 1, 0), 0)),
                  wspec(0), wspec(1), wspec(2), wspec(3),
                  pl.BlockSpec((None, 2 * NH_A, D_MODEL), lambda i, j: (0, n_main // (2 * NH_A), 0)),
                  pl.BlockSpec((1, 2 * NH_A), lambda i, j: (0, 0)),
                  pl.BlockSpec((CONV_A, tn), lambda i, j: (0, j)),
                  pl.BlockSpec((1, tn), lambda i, j: (0, j))],
        out_specs=[ospec, ospec, ospec, pl.BlockSpec((tm, 2 * NH_A), lambda i, j: (i, 0))],
        out_shape=[jax.ShapeDtypeStruct((n, 2 * QK_A), BF16),
                   jax.ShapeDtypeStruct((n, V_A), BF16),
                   jax.ShapeDtypeStruct((n, V_A), BF16),
                   jax.ShapeDtypeStruct((n, 2 * NH_A), F32)],
        scratch_shapes=[pltpu.VMEM((tm, D_MODEL), BF16),
                        pltpu.VMEM((halo, D_MODEL), BF16),
                        pltpu.VMEM((tm + halo, tn), F32)],
        compiler_params=pltpu.CompilerParams(
            dimension_semantics=("arbitrary", "arbitrary"), vmem_limit_bytes=VMEM_LIMIT),
        name="a_in_proj",
    )(x2, x2, w_in_t, w_in_t, w_in_t, w_in_t, w_in_t, gate_b, conv_w, conv_b)


def _mlstm_kernel(q_s, k_s, v_ref, gate_ref, gates_ref, hg_ref, out_ref, st_ref, m_ref, bc_s, ar_s):
    seq = q_s.shape[0]
    L = MLSTM_CHUNK
    HP = MLSTM_HEADS
    nchunk = seq // L
    head0 = pl.program_id(1) * HP

    st_ref[...] = jnp.zeros(st_ref.shape, F32)
    m_ref[...] = jnp.zeros(m_ref.shape, F32)

    ng = 2 * NH_A
    rr = lax.broadcasted_iota(jnp.int32, (L, L), 0)
    cc = lax.broadcasted_iota(jnp.int32, (L, L), 1)
    causal = cc <= rr
    tri = jnp.where(causal, 1.0, 0.0).astype(BF16)
    tri2 = jnp.concatenate([tri, tri], axis=1)
    ones_v = jnp.ones((L, LANES), BF16)
    heads = range(HP)
    qk_l = [slice(h * DK_A, (h + 1) * DK_A) for h in heads]
    v_l = [slice(h * DV_A, (h + 1) * DV_A) for h in heads]

    g = gates_ref[...]
    g_hi, g_lo = _split_bf16(g)
    cum = jnp.concatenate(
        [_dot(tri2, jnp.concatenate([g_hi[c * L:(c + 1) * L], g_lo[c * L:(c + 1) * L]], axis=0))
         for c in range(nchunk)], axis=0)
    c_hi, c_lo = _split_bf16(cum)
    a_hi, a_lo = _split_bf16(g[:, :NH_A] - cum[:, NH_A:])
    a2 = jnp.concatenate([a_hi, a_lo], axis=1)
    eye_r = lax.broadcasted_iota(jnp.int32, (NH_A, ng), 0)
    eye_c = lax.broadcasted_iota(jnp.int32, (NH_A, ng), 1)
    eye2 = jnp.where(eye_c % NH_A == eye_r, 1.0, 0.0).astype(BF16)
    for c in range(nchunk):
        ar_s[c * NH_A:(c + 1) * NH_A, :] = _dot_nt(eye2, a2[c * L:(c + 1) * L])
    c2 = jnp.concatenate([c_hi[:, NH_A:], c_lo[:, NH_A:]], axis=1)
    sel_r = lax.broadcasted_iota(jnp.int32, (ng, 2 * LANES), 0)
    sel_c = lax.broadcasted_iota(jnp.int32, (ng, 2 * LANES), 1)
    for p in range(HP // 2):
        want = head0 + 2 * p + jnp.where(sel_c < LANES, 0, 1)
        onehot = jnp.where(sel_r % NH_A == want, 1.0, 0.0).astype(BF16)
        pair = _dot(c2, onehot)
        bc_s[2 * p] = pair[:, :LANES]
        bc_s[2 * p + 1] = pair[:, LANES:]

    def chunk(c, carry):
        r0 = pl.multiple_of(c * L, L)
        a0 = c * NH_A
        qs = [q_s[pl.ds(r0, L), qk_l[h]] for h in heads]
        ks = [k_s[pl.ds(r0, L), qk_l[h]] for h in heads]
        k_ts = [jnp.transpose(ks[h]) for h in heads]
        vexts = [jnp.concatenate([v_ref[pl.ds(r0, L), v_l[h]], ones_v], axis=1) for h in heads]
        sts = [st_ref[h] for h in heads]
        m_sts = [m_ref[h] for h in heads]
        bcums = [bc_s[h, pl.ds(r0, L), :] for h in heads]
        a_rows = [ar_s[pl.ds(a0 + head0 + h, 1), :] for h in heads]

        qk = [_dot_nt(qs[h], ks[h]) for h in heads]
        carrieds = [_dot(qs[h], sts[h].astype(BF16)) for h in heads]
        gtots = [bc[L - 1:L, :] for bc in bcums]
        dmats = [jnp.where(causal, bcums[h] + a_rows[h], NEG_BIG) for h in heads]
        inters = [bcums[h] + m_sts[h] for h in heads]
        m_qs = [jnp.maximum(inters[h], jnp.max(dmats[h], axis=1, keepdims=True)) for h in heads]
        scores = [(qk[h] * jnp.exp(dmats[h] - m_qs[h])).astype(BF16) for h in heads]
        intras = [_dot(scores[h], vexts[h]) for h in heads]

        for h in heads:
            wlog = gtots[h] + a_rows[h]
            m_new = jnp.maximum(gtots[h] + m_sts[h], jnp.max(wlog, axis=1, keepdims=True))
            wk_row = jnp.exp(wlog - m_new)
            kw_t = (k_ts[h].astype(F32) * wk_row).astype(BF16)
            decay = jnp.exp(gtots[h] + m_sts[h] - m_new)
            st_ref[h] = jnp.concatenate([decay] * 3, axis=1) * sts[h] + _dot(kw_t, vexts[h])
            m_ref[h] = m_new

        for h in heads:
            w_inter = jnp.exp(inters[h] - m_qs[h])
            tot = intras[h] + jnp.concatenate([w_inter] * 3, axis=1) * carrieds[h]
            inv = 1.0 / jnp.maximum(jnp.abs(tot[:, DV_A:]), jnp.exp(-m_qs[h]))
            hv = tot[:, :DV_A] * jnp.concatenate([inv, inv], axis=1)
            mu = jnp.mean(hv, axis=1, keepdims=True)
            hc = hv - mu
            var = jnp.mean(hc * hc, axis=1, keepdims=True)
            hn = hc * lax.rsqrt(var + LN_EPS)
            gate = gate_ref[pl.ds(r0, L), v_l[h]].astype(F32)
            out_ref[pl.ds(r0, L), v_l[h]] = (hn * hg_ref[:, v_l[h]] * gate).astype(BF16)
        return carry

    lax.fori_loop(0, nchunk, chunk, 0)


def _mlstm(qk, v, gate, gates, head_g, bsz, seq):
    n = bsz * seq
    hp = MLSTM_HEADS
    ngrp = NH_A // hp
    return pl.pallas_call(
        _mlstm_kernel,
        grid=(bsz, ngrp),
        in_specs=[pl.BlockSpec((seq, hp * DK_A), lambda b, h: (b, h)),
                  pl.BlockSpec((seq, hp * DK_A), lambda b, h: (b, ngrp + h)),
                  pl.BlockSpec((seq, hp * DV_A), lambda b, h: (b, h)),
                  pl.BlockSpec((seq, hp * DV_A), lambda b, h: (b, h)),
                  pl.BlockSpec((seq, 2 * NH_A), lambda b, h: (b, 0)),
                  pl.BlockSpec((1, hp * DV_A), lambda b, h: (0, h))],
        out_specs=pl.BlockSpec((seq, hp * DV_A), lambda b, h: (b, h)),
        out_shape=jax.ShapeDtypeStruct((n, V_A), BF16),
        scratch_shapes=[pltpu.VMEM((hp, DK_A, DV_A + LANES), F32),
                        pltpu.VMEM((hp, 1, LANES), F32),
                        pltpu.VMEM((hp, seq, LANES), F32),
                        pltpu.VMEM((NH_A * (seq // MLSTM_CHUNK), MLSTM_CHUNK), F32)],
        compiler_params=pltpu.CompilerParams(
            dimension_semantics=("arbitrary", "arbitrary"), vmem_limit_bytes=VMEM_LIMIT),
        name="mlstm",
    )(qk, qk, v, gate, gates, head_g)


def _out_ln_kernel(h_ref, x_ref, w_ref, g_ref, b_ref, *out_refs):
    tm, d = x_ref.shape
    inv_d = 1.0 / d
    for t in range(tm // ROW_SUB):
        rows = slice(t * ROW_SUB, (t + 1) * ROW_SUB)
        hs = h_ref[rows, :]
        s1 = jnp.zeros((ROW_SUB, LANES), F32)
        s2 = jnp.zeros((ROW_SUB, LANES), F32)
        for c in range(d // LN_COL_CHUNK):
            cols = slice(c * LN_COL_CHUNK, (c + 1) * LN_COL_CHUNK)
            r = ALPHA * x_ref[rows, cols] + _dot(hs, w_ref[:, cols])
            out_refs[0][rows, cols] = r
            for l in range(LN_COL_CHUNK // LANES):
                rl = r[:, l * LANES:(l + 1) * LANES]
                s1 = s1 + rl
                s2 = s2 + rl * rl
        mu = jnp.sum(s1, axis=1, keepdims=True) * inv_d
        var = jnp.sum(s2, axis=1, keepdims=True) * inv_d - mu * mu
        rstd = lax.rsqrt(var + LN_EPS)
        for c in range(d // LN_COL_CHUNK):
            cols = slice(c * LN_COL_CHUNK, (c + 1) * LN_COL_CHUNK)
            out = (out_refs[0][rows, cols] - mu) * rstd * g_ref[:, cols] + b_ref[:, cols]
            out_refs[0][rows, cols] = out
            if len(out_refs) > 1:
                out_refs[1][rows, cols] = out.astype(BF16)


def _out_ln(h, x2, w, ln_g, ln_b, with_bf16, tm=512):
    n = x2.shape[0]
    row = pl.BlockSpec((tm, D_MODEL), lambda i: (i, 0))
    vec = pl.BlockSpec((1, D_MODEL), lambda i: (0, 0))
    out_shape = [jax.ShapeDtypeStruct((n, D_MODEL), F32)]
    out_specs = [row]
    if with_bf16:
        out_shape.append(jax.ShapeDtypeStruct((n, D_MODEL), BF16))
        out_specs.append(row)
    return pl.pallas_call(
        _out_ln_kernel,
        grid=(n // tm,),
        in_specs=[row, row, pl.BlockSpec((D_MODEL, D_MODEL), lambda i: (0, 0)), vec, vec],
        out_specs=out_specs,
        out_shape=out_shape,
        compiler_params=pltpu.CompilerParams(
            dimension_semantics=("arbitrary",), vmem_limit_bytes=VMEM_LIMIT),
        name="out_ln",
    )(h, x2, w, ln_g, ln_b)


def _b_proj_kernel(x_ref, wk_ref, wv_ref, wq_ref, wz_ref, k_ref, v_ref, q_ref, zg_ref):
    wk = wk_ref[...].astype(BF16)
    wv = wv_ref[...].astype(BF16)
    wq = wq_ref[...].astype(BF16)
    wz = wz_ref[...].astype(BF16)
    for t in range(x_ref.shape[0] // ROW_SUB):
        rows = slice(t * ROW_SUB, (t + 1) * ROW_SUB)
        xs = x_ref[rows, :]
        k_ref[rows, :] = _dot(xs, wk).astype(BF16)
        v_ref[rows, :] = _dot(xs, wv).astype(BF16)
        q_ref[rows, :] = (_dot(xs, wq) * (DH_B ** -0.5)).astype(BF16)
        z = _dot(xs, wz)
        zg_ref[rows, :] = (z * jax.nn.sigmoid(z)).astype(BF16)


def _b_proj(x1b, kv_w, b_w_in, tm=1024, tn=256):
    n = x1b.shape[0]
    nb = W_B // tn
    wspec = lambda g: pl.BlockSpec((D_MODEL, tn), lambda i, j, g=g: (0, g * nb + j))
    wspec3 = lambda g: pl.BlockSpec((None, D_MODEL, tn), lambda i, j, g=g: (0, 0, g * nb + j))
    ospec = pl.BlockSpec((tm, tn), lambda i, j: (i, j))
    sds = jax.ShapeDtypeStruct((n, W_B), BF16)
    return pl.pallas_call(
        _b_proj_kernel,
        grid=(n // tm, nb),
        in_specs=[pl.BlockSpec((tm, D_MODEL), lambda i, j: (i, 0)), wspec(0), wspec(1), wspec3(0), wspec3(1)],
        out_specs=[ospec] * 4,
        out_shape=[sds] * 4,
        compiler_params=pltpu.CompilerParams(
            dimension_semantics=("arbitrary", "arbitrary"), vmem_limit_bytes=VMEM_LIMIT),
        name="b_proj",
    )(x1b, kv_w, kv_w, b_w_in, b_w_in)


def _sb_kernel(q_ref, k_ref, v_ref, zg_ref, o_ref, acc_ref, carry_ref):
    seq = q_ref.shape[0]
    T = SB_TILE
    HP = q_ref.shape[1] // DH_B
    rr = lax.broadcasted_iota(jnp.int32, (T, T), 0)
    cc = lax.broadcasted_iota(jnp.int32, (T, T), 1)
    below = cc < rr
    neg_suffix = jnp.where(rr > cc, -1.0, 0.0).astype(BF16)
    lanes = [slice(hh * DH_B, (hh + 1) * DH_B) for hh in range(HP)]

    def tiles(q0, k0, diagonal):
        zs = [_dot_nt(q_ref[pl.ds(q0, T), ln], k_ref[pl.ds(k0, T), ln]) for ln in lanes]
        sps, log_betas = [], []
        for z in zs:
            zb = z.astype(BF16)
            sp = jnp.maximum(zb, 0.0) + jnp.log(1.0 + jnp.exp(-jnp.abs(zb)))
            log_betas.append(zb - sp)
            sps.append(jnp.where(below, sp, 0.0) if diagonal else sp)
        betweens = [_dot(sp, neg_suffix) for sp in sps]
        alive = None
        for hh in range(HP):
            shift = betweens[hh] if diagonal else betweens[hh] + carry_ref[hh]
            a = jnp.exp(log_betas[hh] + shift.astype(BF16))
            if diagonal:
                a = jnp.where(below, a, 0.0)
            av = _dot(a, v_ref[pl.ds(k0, T), lanes[hh]])
            acc_ref[hh] = av if diagonal else acc_ref[hh] + av
            total = betweens[hh][:, 0:1] - sps[hh][:, 0:1].astype(F32)
            carry = total if diagonal else carry_ref[hh] + total
            carry_ref[hh] = carry
            alive = carry if alive is None else jnp.maximum(alive, carry)
        return (jnp.max(alive) > SB_DEAD_LOG).astype(jnp.int32)

    def qblock(qi, c):
        q0 = pl.multiple_of(qi * T, T)
        alive = tiles(q0, q0, True)

        def cond(s):
            return jnp.logical_and(s[0] >= 0, s[1] > 0)

        def body(s):
            return s[0] - 1, tiles(q0, pl.multiple_of(s[0] * T, T), False)

        lax.while_loop(cond, body, (qi - 1, alive))
        for hh in range(HP):
            o_ref[pl.ds(q0, T), lanes[hh]] = (acc_ref[hh] * zg_ref[pl.ds(q0, T), lanes[hh]].astype(F32)).astype(BF16)
        return c

    lax.fori_loop(0, seq // T, qblock, 0)


def _stickbreak(q, k, v, zg, bsz, seq):
    n = bsz * seq
    spec = pl.BlockSpec((seq, SB_HEADS * DH_B), lambda b, h: (b, h))
    return pl.pallas_call(
        _sb_kernel,
        grid=(bsz, NH_B // SB_HEADS),
        in_specs=[spec] * 4,
        out_specs=spec,
        out_shape=jax.ShapeDtypeStruct((n, W_B), BF16),
        scratch_shapes=[pltpu.VMEM((SB_HEADS, SB_TILE, DH_B), F32),
                        pltpu.VMEM((SB_HEADS, SB_TILE, 1), F32)],
        compiler_params=pltpu.CompilerParams(
            dimension_semantics=("arbitrary", "arbitrary"), vmem_limit_bytes=VMEM_LIMIT),
        name="stickbreak",
    )(q, k, v, zg)


def kernel(x, a_w_in, a_gate_b, a_conv_w, a_conv_b, a_head_g, a_w_out, a_ln_g, a_ln_b,
           kv_w, b_w_in, b_w_out, b_ln_g, b_ln_b):
    bsz, seq, d = x.shape
    assert d == D_MODEL and a_w_in.shape[0] == 1 and b_w_in.shape[0] == 1
    n = bsz * seq
    x2 = x.reshape(n, d)

    qk, v, gate, gates = _a_in_proj(x2, jnp.swapaxes(a_w_in, 1, 2), a_gate_b[0].reshape(1, -1),
                                    a_conv_w[0], a_conv_b[0].reshape(1, -1), seq)
    hg = _mlstm(qk, v, gate, gates, a_head_g[0].reshape(1, -1), bsz, seq)
    x1, x1b = _out_ln(hg, x2, a_w_out[0].astype(BF16), a_ln_g[0].reshape(1, -1), a_ln_b[0].reshape(1, -1), True)

    k_sh, v_sh, q, zg = _b_proj(x1b, kv_w, b_w_in)
    att = _stickbreak(q, k_sh, v_sh, zg, bsz, seq)
    (out,) = _out_ln(att, x1, b_w_out[0].astype(BF16), b_ln_g[0].reshape(1, -1), b_ln_b[0].reshape(1, -1), False)
    return out.reshape(bsz, seq, d)
```

```python
import functools

import jax
import jax.numpy as jnp
from jax import lax
from jax.experimental import pallas as pl
from jax.experimental.pallas import tpu as pltpu

F32 = jnp.float32
BF16 = jnp.bfloat16

D_MODEL = 2048
NH_A = 8
DK_A = 128
DV_A = 256
QK_A = NH_A * DK_A
V_A = NH_A * DV_A
CONV_A = 4
NH_B = 16
DH_B = 128
W_B = NH_B * DH_B
DEPTH = 2
ALPHA = (2.0 * DEPTH) ** 0.25
LN_EPS = 1e-5

LANES = 128
VMEM_LIMIT = 56 * 1024 * 1024

MLSTM_CHUNK = 128
MLSTM_HEADS = 4
ROW_SUB = 256
LN_COL_CHUNK = 512
SB_TILE = 256
SB_HEADS = 8
SB_DEAD_LOG = -104.0
NEG_BIG = -1e30


def _dot(a, b):
    return jnp.dot(a, b, preferred_element_type=F32)


def _dot_nt(a, b):
    return lax.dot_general(a, b, (((1,), (1,)), ((), ())), preferred_element_type=F32)


def _dot_tn(a, b):
    return lax.dot_general(a, b, (((0,), (0,)), ((), ())), preferred_element_type=F32)


def _split_bf16(x):
    hi = x.astype(BF16)
    lo = (x - hi.astype(F32)).astype(BF16)
    return hi, lo


def _a_in_kernel(x_ref, xh_ref, wqk_ref, wv_ref, wo_ref, wz_ref, wg_ref, gb_ref, cw_ref, cb_ref,
                 qk_ref, v_ref, gate_ref, gates_ref, xb_ref, hb_ref, u_ref, *, tiles_per_seq):
    i = pl.program_id(0)
    j = pl.program_id(1)
    tm = x_ref.shape[0]
    halo = xh_ref.shape[0]

    @pl.when(j == 0)
    def _():
        xb = x_ref[...].astype(BF16)
        xb_ref[...] = xb
        keep = (i % tiles_per_seq != 0).astype(F32)
        hb_ref[...] = (xh_ref[...] * keep).astype(BF16)
        g = _dot_nt(xb, wg_ref[...].astype(BF16)) + gb_ref[...]
        lane = lax.broadcasted_iota(jnp.int32, g.shape, 1)
        log_f = jnp.minimum(g, 0.0) - jnp.log1p(jnp.exp(-jnp.abs(g)))
        gates_ref[...] = jnp.where(lane < NH_A, g, log_f)

    wqk = wqk_ref[...].astype(BF16)
    wv = wv_ref[...].astype(BF16)
    wo = wo_ref[...].astype(BF16)
    wz = wz_ref[...].astype(BF16)
    w = cw_ref[...]
    bias = cb_ref[...]
    scale = jnp.where(j >= pl.num_programs(1) // 2, DK_A ** -0.5, 1.0)
    u_ref[0:halo, :] = _dot_nt(hb_ref[...], wqk)
    for t in range(tm // ROW_SUB):
        rows = slice(t * ROW_SUB, (t + 1) * ROW_SUB)
        xs = xb_ref[rows, :]
        u_ref[halo + t * ROW_SUB:halo + (t + 1) * ROW_SUB, :] = _dot_nt(xs, wqk)
        o = _dot_nt(xs, wo)
        z = _dot_nt(xs, wz)
        gate_ref[rows, :] = (jax.nn.sigmoid(o) * z * jax.nn.sigmoid(z)).astype(BF16)
        acc = jnp.broadcast_to(bias, (ROW_SUB, bias.shape[1]))
        for kk in range(CONV_A):
            off = halo - (CONV_A - 1) + kk + t * ROW_SUB
            acc = acc + w[kk:kk + 1, :] * u_ref[off:off + ROW_SUB, :]
        qk_ref[rows, :] = (acc * jax.nn.sigmoid(acc) * scale).astype(BF16)
        v_ref[rows, :] = _dot_nt(xs, wv).astype(BF16)


def _a_in_proj(x2, w_in_t, gate_b, conv_w, conv_b, seq, tm=1024, tn=256):
    n = x2.shape[0]
    nb = D_MODEL // tn
    halo = 16
    n_main = 2 * QK_A + 3 * V_A
    wspec = lambda g: pl.BlockSpec((None, tn, D_MODEL), lambda i, j, g=g: (0, g * nb + j, 0))
    ospec = pl.BlockSpec((tm, tn), lambda i, j: (i, j))
    return pl.pallas_call(
        functools.partial(_a_in_kernel, tiles_per_seq=seq // tm),
        grid=(n // tm, nb),
        in_specs=[pl.BlockSpec((tm, D_MODEL), lambda i, j: (i, 0)),
                  pl.BlockSpec((halo, D_MODEL), lambda i, j: (jnp.maximum(i * (tm // halo) - 1, 0), 0)),
                  wspec(0), wspec(1), wspec(2), wspec(3),
                  pl.BlockSpec((None, 2 * NH_A, D_MODEL), lambda i, j: (0, n_main // (2 * NH_A), 0)),
                  pl.BlockSpec((1, 2 * NH_A), lambda i, j: (0, 0)),
                  pl.BlockSpec((CONV_A, tn), lambda i, j: (0, j)),
                  pl.BlockSpec((1, tn), lambda i, j: (0, j))],
        out_specs=[ospec, ospec, ospec, pl.BlockSpec((tm, 2 * NH_A), lambda i, j: (i, 0))],
        out_shape=[jax.ShapeDtypeStruct((n, 2 * QK_A), BF16),
                   jax.ShapeDtypeStruct((n, V_A), BF16),
                   jax.ShapeDtypeStruct((n, V_A), BF16),
                   jax.ShapeDtypeStruct((n, 2 * NH_A), F32)],
        scratch_shapes=[pltpu.VMEM((tm, D_MODEL), BF16),
                        pltpu.VMEM((halo, D_MODEL), BF16),
                        pltpu.VMEM((tm + halo, tn), F32)],
        compiler_params=pltpu.CompilerParams(
            dimension_semantics=("arbitrary", "arbitrary"), vmem_limit_bytes=VMEM_LIMIT),
        name="a_in_proj",
    )(x2, x2, w_in_t, w_in_t, w_in_t, w_in_t, w_in_t, gate_b, conv_w, conv_b)


def _mlstm_kernel(q_s, k_s, v_ref, gate_ref, gates_ref, hg_ref, out_ref, st_ref, m_ref, bc_s, ar_s):
    seq = q_s.shape[0]
    L = MLSTM_CHUNK
    HP = MLSTM_HEADS
    nchunk = seq // L
    head0 = pl.program_id(1) * HP

    st_ref[...] = jnp.zeros(st_ref.shape, F32)
    m_ref[...] = jnp.zeros(m_ref.shape, F32)

    ng = 2 * NH_A
    rr = lax.broadcasted_iota(jnp.int32, (L, L), 0)
    cc = lax.broadcasted_iota(jnp.int32, (L, L), 1)
    causal = cc <= rr
    tri = jnp.where(causal, 1.0, 0.0).astype(BF16)
    tri2 = jnp.concatenate([tri, tri], axis=1)
    ones_v = jnp.ones((L, LANES), BF16)
    heads = range(HP)
    qk_l = [slice(h * DK_A, (h + 1) * DK_A) for h in heads]
    v_l = [slice(h * DV_A, (h + 1) * DV_A) for h in heads]

    g = gates_ref[...]
    g_hi, g_lo = _split_bf16(g)
    cum = jnp.concatenate(
        [_dot(tri2, jnp.concatenate([g_hi[c * L:(c + 1) * L], g_lo[c * L:(c + 1) * L]], axis=0))
         for c in range(nchunk)], axis=0)
    c_hi, c_lo = _split_bf16(cum)
    a_hi, a_lo = _split_bf16(g[:, :NH_A] - cum[:, NH_A:])
    a2 = jnp.concatenate([a_hi, a_lo], axis=1)
    eye_r = lax.broadcasted_iota(jnp.int32, (NH_A, ng), 0)
    eye_c = lax.broadcasted_iota(jnp.int32, (NH_A, ng), 1)
    eye2 = jnp.where(eye_c % NH_A == eye_r, 1.0, 0.0).astype(BF16)
    for c in range(nchunk):
        ar_s[c * NH_A:(c + 1) * NH_A, :] = _dot_nt(eye2, a2[c * L:(c + 1) * L])
    c2 = jnp.concatenate([c_hi[:, NH_A:], c_lo[:, NH_A:]], axis=1)
    sel_r = lax.broadcasted_iota(jnp.int32, (ng, 2 * LANES), 0)
    sel_c = lax.broadcasted_iota(jnp.int32, (ng, 2 * LANES), 1)
    for p in range(HP // 2):
        want = head0 + 2 * p + jnp.where(sel_c < LANES, 0, 1)
        onehot = jnp.where(sel_r % NH_A == want, 1.0, 0.0).astype(BF16)
        pair = _dot(c2, onehot)
        bc_s[2 * p] = pair[:, :LANES]
        bc_s[2 * p + 1] = pair[:, LANES:]

    def chunk(c, carry):
        r0 = pl.multiple_of(c * L, L)
        a0 = c * NH_A
        qs = [q_s[pl.ds(r0, L), qk_l[h]] for h in heads]
        ks = [k_s[pl.ds(r0, L), qk_l[h]] for h in heads]
        k_ts = [jnp.transpose(ks[h]) for h in heads]
        vexts = [jnp.concatenate([v_ref[pl.ds(r0, L), v_l[h]], ones_v], axis=1) for h in heads]
        sts = [st_ref[h] for h in heads]
        m_sts = [m_ref[h] for h in heads]
        bcums = [bc_s[h, pl.ds(r0, L), :] for h in heads]
        a_rows = [ar_s[pl.ds(a0 + head0 + h, 1), :] for h in heads]

        qk = [_dot_nt(qs[h], ks[h]) for h in heads]
        carrieds = [_dot(qs[h], sts[h].astype(BF16)) for h in heads]
        gtots = [bc[L - 1:L, :] for bc in bcums]
        dmats = [jnp.where(causal, bcums[h] + a_rows[h], NEG_BIG) for h in heads]
        inters = [bcums[h] + m_sts[h] for h in heads]
        m_qs = [jnp.maximum(inters[h], jnp.max(dmats[h], axis=1, keepdims=True)) for h in heads]
        scores = [(qk[h] * jnp.exp(dmats[h] - m_qs[h])).astype(BF16) for h in heads]
        intras = [_dot(scores[h], vexts[h]) for h in heads]

        for h in heads:
            wlog = gtots[h] + a_rows[h]
            m_new = jnp.maximum(gtots[h] + m_sts[h], jnp.max(wlog, axis=1, keepdims=True))
            wk_row = jnp.exp(wlog - m_new)
            kw_t = (k_ts[h].astype(F32) * wk_row).astype(BF16)
            decay = jnp.exp(gtots[h] + m_sts[h] - m_new)
            st_ref[h] = jnp.concatenate([decay] * 3, axis=1) * sts[h] + _dot(kw_t, vexts[h])
            m_ref[h] = m_new

        for h in heads:
            w_inter = jnp.exp(inters[h] - m_qs[h])
            tot = intras[h] + jnp.concatenate([w_inter] * 3, axis=1) * carrieds[h]
            inv = 1.0 / jnp.maximum(jnp.abs(tot[:, DV_A:]), jnp.exp(-m_qs[h]))
            hv = tot[:, :DV_A] * jnp.concatenate([inv, inv], axis=1)
            mu = jnp.mean(hv, axis=1, keepdims=True)
            hc = hv - mu
            var = jnp.mean(hc * hc, axis=1, keepdims=True)
            hn = hc * lax.rsqrt(var + LN_EPS)
            gate = gate_ref[pl.ds(r0, L), v_l[h]].astype(F32)
            out_ref[pl.ds(r0, L), v_l[h]] = (hn * hg_ref[:, v_l[h]] * gate).astype(BF16)
        return carry

    lax.fori_loop(0, nchunk, chunk, 0)


def _mlstm(qk, v, gate, gates, head_g, bsz, seq):
    n = bsz * seq
    hp = MLSTM_HEADS
    ngrp = NH_A // hp
    return pl.pallas_call(
        _mlstm_kernel,
        grid=(bsz, ngrp),
        in_specs=[pl.BlockSpec((seq, hp * DK_A), lambda b, h: (b, h)),
                  pl.BlockSpec((seq, hp * DK_A), lambda b, h: (b, ngrp + h)),
                  pl.BlockSpec((seq, hp * DV_A), lambda b, h: (b, h)),
                  pl.BlockSpec((seq, hp * DV_A), lambda b, h: (b, h)),
                  pl.BlockSpec((seq, 2 * NH_A), lambda b, h: (b, 0)),
                  pl.BlockSpec((1, hp * DV_A), lambda b, h: (0, h))],
        out_specs=pl.BlockSpec((seq, hp * DV_A), lambda b, h: (b, h)),
        out_shape=jax.ShapeDtypeStruct((n, V_A), BF16),
        scratch_shapes=[pltpu.VMEM((hp, DK_A, DV_A + LANES), F32),
                        pltpu.VMEM((hp, 1, LANES), F32),
                        pltpu.VMEM((hp, seq, LANES), F32),
                        pltpu.VMEM((NH_A * (seq // MLSTM_CHUNK), MLSTM_CHUNK), F32)],
        compiler_params=pltpu.CompilerParams(
            dimension_semantics=("arbitrary", "arbitrary"), vmem_limit_bytes=VMEM_LIMIT),
        name="mlstm",
    )(qk, qk, v, gate, gates, head_g)


def _out_ln_kernel(h_ref, x_ref, w_ref, g_ref, b_ref, *out_refs):
    tm, d = x_ref.shape
    inv_d = 1.0 / d
    for t in range(tm // ROW_SUB):
        rows = slice(t * ROW_SUB, (t + 1) * ROW_SUB)
        hs = h_ref[rows, :]
        s1 = jnp.zeros((ROW_SUB, LANES), F32)
        s2 = jnp.zeros((ROW_SUB, LANES), F32)
        for c in range(d // LN_COL_CHUNK):
            cols = slice(c * LN_COL_CHUNK, (c + 1) * LN_COL_CHUNK)
            r = ALPHA * x_ref[rows, cols] + _dot(hs, w_ref[:, cols])
            out_refs[0][rows, cols] = r
            for l in range(LN_COL_CHUNK // LANES):
                rl = r[:, l * LANES:(l + 1) * LANES]
                s1 = s1 + rl
                s2 = s2 + rl * rl
        mu = jnp.sum(s1, axis=1, keepdims=True) * inv_d
        var = jnp.sum(s2, axis=1, keepdims=True) * inv_d - mu * mu
        rstd = lax.rsqrt(var + LN_EPS)
        for c in range(d // LN_COL_CHUNK):
            cols = slice(c * LN_COL_CHUNK, (c + 1) * LN_COL_CHUNK)
            out = (out_refs[0][rows, cols] - mu) * rstd * g_ref[:, cols] + b_ref[:, cols]
            out_refs[0][rows, cols] = out
            if len(out_refs) > 1:
                out_refs[1][rows, cols] = out.astype(BF16)


def _out_ln(h, x2, w, ln_g, ln_b, with_bf16, tm=512):
    n = x2.shape[0]
    row = pl.BlockSpec((tm, D_MODEL), lambda i: (i, 0))
    vec = pl.BlockSpec((1, D_MODEL), lambda i: (0, 0))
    out_shape = [jax.ShapeDtypeStruct((n, D_MODEL), F32)]
    out_specs = [row]
    if with_bf16:
        out_shape.append(jax.ShapeDtypeStruct((n, D_MODEL), BF16))
        out_specs.append(row)
    return pl.pallas_call(
        _out_ln_kernel,
        grid=(n // tm,),
        in_specs=[row, row, pl.BlockSpec((D_MODEL, D_MODEL), lambda i: (0, 0)), vec, vec],
        out_specs=out_specs,
        out_shape=out_shape,
        compiler_params=pltpu.CompilerParams(
            dimension_semantics=("arbitrary",), vmem_limit_bytes=VMEM_LIMIT),
        name="out_ln",
    )(h, x2, w, ln_g, ln_b)


def _b_proj_kernel(x_ref, wk_ref, wv_ref, wq_ref, wz_ref, k_ref, v_ref, q_ref, zg_ref):
    wk = wk_ref[...].astype(BF16)
    wv = wv_ref[...].astype(BF16)
    wq = wq_ref[...].astype(BF16)
    wz = wz_ref[...].astype(BF16)
    for t in range(x_ref.shape[0] // ROW_SUB):
        rows = slice(t * ROW_SUB, (t + 1) * ROW_SUB)
        xs = x_ref[rows, :]
        k_ref[rows, :] = _dot(xs, wk).astype(BF16)
        v_ref[rows, :] = _dot(xs, wv).astype(BF16)
        q_ref[rows, :] = (_dot(xs, wq) * (DH_B ** -0.5)).astype(BF16)
        z = _dot(xs, wz)
        zg_ref[rows, :] = (z * jax.nn.sigmoid(z)).astype(BF16)


def _b_proj(x1b, kv_w, b_w_in, tm=2048, tn=256):
    n = x1b.shape[0]
    nb = W_B // tn
    wspec = lambda g: pl.BlockSpec((D_MODEL, tn), lambda i, j, g=g: (0, g * nb + j))
    wspec3 = lambda g: pl.BlockSpec((None, D_MODEL, tn), lambda i, j, g=g: (0, 0, g * nb + j))
    ospec = pl.BlockSpec((tm, tn), lambda i, j: (i, j))
    sds = jax.ShapeDtypeStruct((n, W_B), BF16)
    return pl.pallas_call(
        _b_proj_kernel,
        grid=(n // tm, nb),
        in_specs=[pl.BlockSpec((tm, D_MODEL), lambda i, j: (i, 0)), wspec(0), wspec(1), wspec3(0), wspec3(1)],
        out_specs=[ospec] * 4,
        out_shape=[sds] * 4,
        compiler_params=pltpu.CompilerParams(
            dimension_semantics=("arbitrary", "arbitrary"), vmem_limit_bytes=VMEM_LIMIT),
        name="b_proj",
    )(x1b, kv_w, kv_w, b_w_in, b_w_in)


def _sb_kernel(q_ref, k_ref, v_ref, zg_ref, o_ref, acc_ref, carry_ref):
    seq = q_ref.shape[0]
    T = SB_TILE
    HP = q_ref.shape[1] // DH_B
    rr = lax.broadcasted_iota(jnp.int32, (T, T), 0)
    cc = lax.broadcasted_iota(jnp.int32, (T, T), 1)
    below = cc < rr
    neg_suffix = jnp.where(rr > cc, -1.0, 0.0).astype(BF16)
    lanes = [slice(hh * DH_B, (hh + 1) * DH_B) for hh in range(HP)]

    def tiles(q0, k0, diagonal):
        zs = [_dot_nt(q_ref[pl.ds(q0, T), ln], k_ref[pl.ds(k0, T), ln]) for ln in lanes]
        sps, log_betas = [], []
        for z in zs:
            zb = z.astype(BF16)
            sp = jnp.maximum(zb, 0.0) + jnp.log(1.0 + jnp.exp(-jnp.abs(zb)))
            log_betas.append(zb - sp)
            sps.append(jnp.where(below, sp, 0.0) if diagonal else sp)
        betweens = [_dot(sp, neg_suffix) for sp in sps]
        alive = None
        for hh in range(HP):
            shift = betweens[hh] if diagonal else betweens[hh] + carry_ref[hh]
            a = jnp.exp(log_betas[hh] + shift.astype(BF16))
            if diagonal:
                a = jnp.where(below, a, 0.0)
            av = _dot(a, v_ref[pl.ds(k0, T), lanes[hh]])
            acc_ref[hh] = av if diagonal else acc_ref[hh] + av
            total = betweens[hh][:, 0:1] - sps[hh][:, 0:1].astype(F32)
            carry = total if diagonal else carry_ref[hh] + total
            carry_ref[hh] = carry
            alive = carry if alive is None else jnp.maximum(alive, carry)
        return (jnp.max(alive) > SB_DEAD_LOG).astype(jnp.int32)

    def qblock(qi, c):
        q0 = pl.multiple_of(qi * T, T)
        alive = tiles(q0, q0, True)

        def cond(s):
            return jnp.logical_and(s[0] >= 0, s[1] > 0)

        def body(s):
            return s[0] - 1, tiles(q0, pl.multiple_of(s[0] * T, T), False)

        lax.while_loop(cond, body, (qi - 1, alive))
        for hh in range(HP):
            o_ref[pl.ds(q0, T), lanes[hh]] = (acc_ref[hh] * zg_ref[pl.ds(q0, T), lanes[hh]].astype(F32)).astype(BF16)
        return c

    lax.fori_loop(0, seq // T, qblock, 0)


def _stickbreak(q, k, v, zg, bsz, seq):
    n = bsz * seq
    spec = pl.BlockSpec((seq, SB_HEADS * DH_B), lambda b, h: (b, h))
    return pl.pallas_call(
        _sb_kernel,
        grid=(bsz, NH_B // SB_HEADS),
        in_specs=[spec] * 4,
        out_specs=spec,
        out_shape=jax.ShapeDtypeStruct((n, W_B), BF16),
        scratch_shapes=[pltpu.VMEM((SB_HEADS, SB_TILE, DH_B), F32),
                        pltpu.VMEM((SB_HEADS, SB_TILE, 1), F32)],
        compiler_params=pltpu.CompilerParams(
            dimension_semantics=("arbitrary", "arbitrary"), vmem_limit_bytes=VMEM_LIMIT),
        name="stickbreak",
    )(q, k, v, zg)


def kernel(x, a_w_in, a_gate_b, a_conv_w, a_conv_b, a_head_g, a_w_out, a_ln_g, a_ln_b,
           kv_w, b_w_in, b_w_out, b_ln_g, b_ln_b):
    bsz, seq, d = x.shape
    assert d == D_MODEL and a_w_in.shape[0] == 1 and b_w_in.shape[0] == 1
    n = bsz * seq
    x2 = x.reshape(n, d)

    qk, v, gate, gates = _a_in_proj(x2, jnp.swapaxes(a_w_in, 1, 2), a_gate_b[0].reshape(1, -1),
                                    a_conv_w[0], a_conv_b[0].reshape(1, -1), seq)
    hg = _mlstm(qk, v, gate, gates, a_head_g[0].reshape(1, -1), bsz, seq)
    x1, x1b = _out_ln(hg, x2, a_w_out[0].astype(BF16), a_ln_g[0].reshape(1, -1), a_ln_b[0].reshape(1, -1), True)

    k_sh, v_sh, q, zg = _b_proj(x1b, kv_w, b_w_in)
    att = _stickbreak(q, k_sh, v_sh, zg, bsz, seq)
    (out,) = _out_ln(att, x1, b_w_out[0].astype(BF16), b_ln_g[0].reshape(1, -1), b_ln_b[0].reshape(1, -1), False)
    return out.reshape(bsz, seq, d)
```

```python
import functools

import jax
import jax.numpy as jnp
from jax import lax
from jax.experimental import pallas as pl
from jax.experimental.pallas import tpu as pltpu

F32 = jnp.float32
BF16 = jnp.bfloat16

D_MODEL = 2048
NH_A = 8
DK_A = 128
DV_A = 256
QK_A = NH_A * DK_A
V_A = NH_A * DV_A
CONV_A = 4
NH_B = 16
DH_B = 128
W_B = NH_B * DH_B
DEPTH = 2
ALPHA = (2.0 * DEPTH) ** 0.25
LN_EPS = 1e-5

LANES = 128
VMEM_LIMIT = 56 * 1024 * 1024

MLSTM_CHUNK = 128
MLSTM_HEADS = 4
ROW_SUB = 256
LN_COL_CHUNK = 512
SB_TILE = 256
SB_HEADS = 8
SB_DEAD_LOG = -104.0
NEG_BIG = -1e30


def _dot(a, b):
    return jnp.dot(a, b, preferred_element_type=F32)


def _dot_nt(a, b):
    return lax.dot_general(a, b, (((1,), (1,)), ((), ())), preferred_element_type=F32)


def _dot_tn(a, b):
    return lax.dot_general(a, b, (((0,), (0,)), ((), ())), preferred_element_type=F32)


def _split_bf16(x):
    hi = x.astype(BF16)
    lo = (x - hi.astype(F32)).astype(BF16)
    return hi, lo


def _a_in_kernel(x_ref, xh_ref, wqk_ref, wv_ref, wo_ref, wz_ref, wg_ref, gb_ref, cw_ref, cb_ref,
                 qk_ref, v_ref, gate_ref, gates_ref, xb_ref, hb_ref, u_ref, wt_ref, *, tiles_per_seq):
    i = pl.program_id(0)
    j = pl.program_id(1)
    tm = x_ref.shape[0]
    halo = xh_ref.shape[0]

    @pl.when(j == 0)
    def _():
        xb = x_ref[...].astype(BF16)
        xb_ref[...] = xb
        keep = (i % tiles_per_seq != 0).astype(F32)
        hb_ref[...] = (xh_ref[...] * keep).astype(BF16)
        g = _dot_nt(xb, wg_ref[...].astype(BF16)) + gb_ref[...]
        lane = lax.broadcasted_iota(jnp.int32, g.shape, 1)
        log_f = jnp.minimum(g, 0.0) - jnp.log1p(jnp.exp(-jnp.abs(g)))
        gates_ref[...] = jnp.where(lane < NH_A, g, log_f)

    for n, ref in enumerate((wqk_ref, wo_ref, wz_ref, wv_ref)):
        wt_ref[n] = jnp.transpose(ref[...].astype(BF16))
    wqk = wt_ref[0]
    wo = wt_ref[1]
    wz = wt_ref[2]
    wv = wt_ref[3]
    w = cw_ref[...]
    bias = cb_ref[...]
    scale = jnp.where(j >= pl.num_programs(1) // 2, DK_A ** -0.5, 1.0)
    u_ref[0:halo, :] = _dot(hb_ref[...], wqk)
    for t in range(tm // ROW_SUB):
        rows = slice(t * ROW_SUB, (t + 1) * ROW_SUB)
        xs = xb_ref[rows, :]
        u_ref[halo + t * ROW_SUB:halo + (t + 1) * ROW_SUB, :] = _dot(xs, wqk)
        o = _dot(xs, wo)
        z = _dot(xs, wz)
        gate_ref[rows, :] = (jax.nn.sigmoid(o) * z * jax.nn.sigmoid(z)).astype(BF16)
        acc = jnp.broadcast_to(bias, (ROW_SUB, bias.shape[1]))
        for kk in range(CONV_A):
            off = halo - (CONV_A - 1) + kk + t * ROW_SUB
            acc = acc + w[kk:kk + 1, :] * u_ref[off:off + ROW_SUB, :]
        qk_ref[rows, :] = (acc * jax.nn.sigmoid(acc) * scale).astype(BF16)
        v_ref[rows, :] = _dot(xs, wv).astype(BF16)


def _a_in_proj(x2, w_in_t, gate_b, conv_w, conv_b, seq, tm=1024, tn=256):
    n = x2.shape[0]
    nb = D_MODEL // tn
    halo = 16
    n_main = 2 * QK_A + 3 * V_A
    wspec = lambda g: pl.BlockSpec((None, tn, D_MODEL), lambda i, j, g=g: (0, g * nb + j, 0))
    ospec = pl.BlockSpec((tm, tn), lambda i, j: (i, j))
    return pl.pallas_call(
        functools.partial(_a_in_kernel, tiles_per_seq=seq // tm),
        grid=(n // tm, nb),
        in_specs=[pl.BlockSpec((tm, D_MODEL), lambda i, j: (i, 0)),
                  pl.BlockSpec((halo, D_MODEL), lambda i, j: (jnp.maximum(i * (tm // halo) - 1, 0), 0)),
                  wspec(0), wspec(1), wspec(2), wspec(3),
                  pl.BlockSpec((None, 2 * NH_A, D_MODEL), lambda i, j: (0, n_main // (2 * NH_A), 0)),
                  pl.BlockSpec((1, 2 * NH_A), lambda i, j: (0, 0)),
                  pl.BlockSpec((CONV_A, tn), lambda i, j: (0, j)),
                  pl.BlockSpec((1, tn), lambda i, j: (0, j))],
        out_specs=[ospec, ospec, ospec, pl.BlockSpec((tm, 2 * NH_A), lambda i, j: (i, 0))],
        out_shape=[jax.ShapeDtypeStruct((n, 2 * QK_A), BF16),
                   jax.ShapeDtypeStruct((n, V_A), BF16),
                   jax.ShapeDtypeStruct((n, V_A), BF16),
                   jax.ShapeDtypeStruct((n, 2 * NH_A), F32)],
        scratch_shapes=[pltpu.VMEM((tm, D_MODEL), BF16),
                        pltpu.VMEM((halo, D_MODEL), BF16),
                        pltpu.VMEM((tm + halo, tn), F32),
                        pltpu.VMEM((4, D_MODEL, tn), BF16)],
        compiler_params=pltpu.CompilerParams(
            dimension_semantics=("arbitrary", "arbitrary"), vmem_limit_bytes=VMEM_LIMIT),
        name="a_in_proj",
    )(x2, x2, w_in_t, w_in_t, w_in_t, w_in_t, w_in_t, gate_b, conv_w, conv_b)


def _mlstm_kernel(q_s, k_s, v_ref, gate_ref, gates_ref, hg_ref, out_ref, st_ref, m_ref, bc_s, ar_s):
    seq = q_s.shape[0]
    L = MLSTM_CHUNK
    HP = MLSTM_HEADS
    nchunk = seq // L
    head0 = pl.program_id(1) * HP

    st_ref[...] = jnp.zeros(st_ref.shape, F32)
    m_ref[...] = jnp.zeros(m_ref.shape, F32)

    ng = 2 * NH_A
    rr = lax.broadcasted_iota(jnp.int32, (L, L), 0)
    cc = lax.broadcasted_iota(jnp.int32, (L, L), 1)
    causal = cc <= rr
    tri = jnp.where(causal, 1.0, 0.0).astype(BF16)
    tri2 = jnp.concatenate([tri, tri], axis=1)
    ones_v = jnp.ones((L, LANES), BF16)
    heads = range(HP)
    qk_l = [slice(h * DK_A, (h + 1) * DK_A) for h in heads]
    v_l = [slice(h * DV_A, (h + 1) * DV_A) for h in heads]

    g = gates_ref[...]
    g_hi, g_lo = _split_bf16(g)
    cum = jnp.concatenate(
        [_dot(tri2, jnp.concatenate([g_hi[c * L:(c + 1) * L], g_lo[c * L:(c + 1) * L]], axis=0))
         for c in range(nchunk)], axis=0)
    c_hi, c_lo = _split_bf16(cum)
    a_hi, a_lo = _split_bf16(g[:, :NH_A] - cum[:, NH_A:])
    a2 = jnp.concatenate([a_hi, a_lo], axis=1)
    eye_r = lax.broadcasted_iota(jnp.int32, (NH_A, ng), 0)
    eye_c = lax.broadcasted_iota(jnp.int32, (NH_A, ng), 1)
    eye2 = jnp.where(eye_c % NH_A == eye_r, 1.0, 0.0).astype(BF16)
    for c in range(nchunk):
        ar_s[c * NH_A:(c + 1) * NH_A, :] = _dot_nt(eye2, a2[c * L:(c + 1) * L])
    c2 = jnp.concatenate([c_hi[:, NH_A:], c_lo[:, NH_A:]], axis=1)
    sel_r = lax.broadcasted_iota(jnp.int32, (ng, 2 * LANES), 0)
    sel_c = lax.broadcasted_iota(jnp.int32, (ng, 2 * LANES), 1)
    for p in range(HP // 2):
        want = head0 + 2 * p + jnp.where(sel_c < LANES, 0, 1)
        onehot = jnp.where(sel_r % NH_A == want, 1.0, 0.0).astype(BF16)
        pair = _dot(c2, onehot)
        bc_s[2 * p] = pair[:, :LANES]
        bc_s[2 * p + 1] = pair[:, LANES:]

    def chunk(c, carry):
        r0 = pl.multiple_of(c * L, L)
        a0 = c * NH_A
        qs = [q_s[pl.ds(r0, L), qk_l[h]] for h in heads]
        ks = [k_s[pl.ds(r0, L), qk_l[h]] for h in heads]
        k_ts = [jnp.transpose(ks[h]) for h in heads]
        vexts = [jnp.concatenate([v_ref[pl.ds(r0, L), v_l[h]], ones_v], axis=1) for h in heads]
        sts = [st_ref[h] for h in heads]
        m_sts = [m_ref[h] for h in heads]
        bcums = [bc_s[h, pl.ds(r0, L), :] for h in heads]
        a_rows = [ar_s[pl.ds(a0 + head0 + h, 1), :] for h in heads]

        qk = [_dot_nt(qs[h], ks[h]) for h in heads]
        carrieds = [_dot(qs[h], sts[h].astype(BF16)) for h in heads]
        gtots = [bc[L - 1:L, :] for bc in bcums]
        dmats = [jnp.where(causal, bcums[h] + a_rows[h], NEG_BIG) for h in heads]
        inters = [bcums[h] + m_sts[h] for h in heads]
        m_qs = [jnp.maximum(inters[h], jnp.max(dmats[h], axis=1, keepdims=True)) for h in heads]
        scores = [(qk[h] * jnp.exp(dmats[h] - m_qs[h])).astype(BF16) for h in heads]
        intras = [_dot(scores[h], vexts[h]) for h in heads]

        for h in heads:
            wlog = gtots[h] + a_rows[h]
            m_new = jnp.maximum(gtots[h] + m_sts[h], jnp.max(wlog, axis=1, keepdims=True))
            wk_row = jnp.exp(wlog - m_new)
            kw_t = (k_ts[h].astype(F32) * wk_row).astype(BF16)
            decay = jnp.exp(gtots[h] + m_sts[h] - m_new)
            st_ref[h] = jnp.concatenate([decay] * 3, axis=1) * sts[h] + _dot(kw_t, vexts[h])
            m_ref[h] = m_new

        for h in heads:
            w_inter = jnp.exp(inters[h] - m_qs[h])
            tot = intras[h] + jnp.concatenate([w_inter] * 3, axis=1) * carrieds[h]
            inv = 1.0 / jnp.maximum(jnp.abs(tot[:, DV_A:]), jnp.exp(-m_qs[h]))
            hv = tot[:, :DV_A] * jnp.concatenate([inv, inv], axis=1)
            mu = jnp.mean(hv, axis=1, keepdims=True)
            hc = hv - mu
            var = jnp.mean(hc * hc, axis=1, keepdims=True)
            hn = hc * lax.rsqrt(var + LN_EPS)
            gate = gate_ref[pl.ds(r0, L), v_l[h]].astype(F32)
            out_ref[pl.ds(r0, L), v_l[h]] = (hn * hg_ref[:, v_l[h]] * gate).astype(BF16)
        return carry

    lax.fori_loop(0, nchunk, chunk, 0)


def _mlstm(qk, v, gate, gates, head_g, bsz, seq):
    n = bsz * seq
    hp = MLSTM_HEADS
    ngrp = NH_A // hp
    return pl.pallas_call(
        _mlstm_kernel,
        grid=(bsz, ngrp),
        in_specs=[pl.BlockSpec((seq, hp * DK_A), lambda b, h: (b, h)),
                  pl.BlockSpec((seq, hp * DK_A), lambda b, h: (b, ngrp + h)),
                  pl.BlockSpec((seq, hp * DV_A), lambda b, h: (b, h)),
                  pl.BlockSpec((seq, hp * DV_A), lambda b, h: (b, h)),
                  pl.BlockSpec((seq, 2 * NH_A), lambda b, h: (b, 0)),
                  pl.BlockSpec((1, hp * DV_A), lambda b, h: (0, h))],
        out_specs=pl.BlockSpec((seq, hp * DV_A), lambda b, h: (b, h)),
        out_shape=jax.ShapeDtypeStruct((n, V_A), BF16),
        scratch_shapes=[pltpu.VMEM((hp, DK_A, DV_A + LANES), F32),
                        pltpu.VMEM((hp, 1, LANES), F32),
                        pltpu.VMEM((hp, seq, LANES), F32),
                        pltpu.VMEM((NH_A * (seq // MLSTM_CHUNK), MLSTM_CHUNK), F32)],
        compiler_params=pltpu.CompilerParams(
            dimension_semantics=("arbitrary", "arbitrary"), vmem_limit_bytes=VMEM_LIMIT),
        name="mlstm",
    )(qk, qk, v, gate, gates, head_g)


def _out_ln_kernel(h_ref, x_ref, w_ref, g_ref, b_ref, *out_refs):
    tm, d = x_ref.shape
    inv_d = 1.0 / d
    for t in range(tm // ROW_SUB):
        rows = slice(t * ROW_SUB, (t + 1) * ROW_SUB)
        hs = h_ref[rows, :]
        s1 = jnp.zeros((ROW_SUB, LANES), F32)
        s2 = jnp.zeros((ROW_SUB, LANES), F32)
        for c in range(d // LN_COL_CHUNK):
            cols = slice(c * LN_COL_CHUNK, (c + 1) * LN_COL_CHUNK)
            r = ALPHA * x_ref[rows, cols] + _dot(hs, w_ref[:, cols])
            out_refs[0][rows, cols] = r
            for l in range(LN_COL_CHUNK // LANES):
                rl = r[:, l * LANES:(l + 1) * LANES]
                s1 = s1 + rl
                s2 = s2 + rl * rl
        mu = jnp.sum(s1, axis=1, keepdims=True) * inv_d
        var = jnp.sum(s2, axis=1, keepdims=True) * inv_d - mu * mu
        rstd = lax.rsqrt(var + LN_EPS)
        for c in range(d // LN_COL_CHUNK):
            cols = slice(c * LN_COL_CHUNK, (c + 1) * LN_COL_CHUNK)
            out = (out_refs[0][rows, cols] - mu) * rstd * g_ref[:, cols] + b_ref[:, cols]
            out_refs[0][rows, cols] = out
            if len(out_refs) > 1:
                out_refs[1][rows, cols] = out.astype(BF16)


def _out_ln(h, x2, w, ln_g, ln_b, with_bf16, tm=512):
    n = x2.shape[0]
    row = pl.BlockSpec((tm, D_MODEL), lambda i: (i, 0))
    vec = pl.BlockSpec((1, D_MODEL), lambda i: (0, 0))
    out_shape = [jax.ShapeDtypeStruct((n, D_MODEL), F32)]
    out_specs = [row]
    if with_bf16:
        out_shape.append(jax.ShapeDtypeStruct((n, D_MODEL), BF16))
        out_specs.append(row)
    return pl.pallas_call(
        _out_ln_kernel,
        grid=(n // tm,),
        in_specs=[row, row, pl.BlockSpec((D_MODEL, D_MODEL), lambda i: (0, 0)), vec, vec],
        out_specs=out_specs,
        out_shape=out_shape,
        compiler_params=pltpu.CompilerParams(
            dimension_semantics=("arbitrary",), vmem_limit_bytes=VMEM_LIMIT),
        name="out_ln",
    )(h, x2, w, ln_g, ln_b)


def _b_proj_kernel(x_ref, wk_ref, wv_ref, wq_ref, wz_ref, k_ref, v_ref, q_ref, zg_ref):
    wk = wk_ref[...].astype(BF16)
    wv = wv_ref[...].astype(BF16)
    wq = wq_ref[...].astype(BF16)
    wz = wz_ref[...].astype(BF16)
    for t in range(x_ref.shape[0] // ROW_SUB):
        rows = slice(t * ROW_SUB, (t + 1) * ROW_SUB)
        xs = x_ref[rows, :]
        k_ref[rows, :] = _dot(xs, wk).astype(BF16)
        v_ref[rows, :] = _dot(xs, wv).astype(BF16)
        q_ref[rows, :] = (_dot(xs, wq) * (DH_B ** -0.5)).astype(BF16)
        z = _dot(xs, wz)
        zg_ref[rows, :] = (z * jax.nn.sigmoid(z)).astype(BF16)


def _b_proj(x1b, kv_w, b_w_in, tm=2048, tn=256):
    n = x1b.shape[0]
    nb = W_B // tn
    wspec = lambda g: pl.BlockSpec((D_MODEL, tn), lambda i, j, g=g: (0, g * nb + j))
    wspec3 = lambda g: pl.BlockSpec((None, D_MODEL, tn), lambda i, j, g=g: (0, 0, g * nb + j))
    ospec = pl.BlockSpec((tm, tn), lambda i, j: (i, j))
    sds = jax.ShapeDtypeStruct((n, W_B), BF16)
    return pl.pallas_call(
        _b_proj_kernel,
        grid=(n // tm, nb),
        in_specs=[pl.BlockSpec((tm, D_MODEL), lambda i, j: (i, 0)), wspec(0), wspec(1), wspec3(0), wspec3(1)],
        out_specs=[ospec] * 4,
        out_shape=[sds] * 4,
        compiler_params=pltpu.CompilerParams(
            dimension_semantics=("arbitrary", "arbitrary"), vmem_limit_bytes=VMEM_LIMIT),
        name="b_proj",
    )(x1b, kv_w, kv_w, b_w_in, b_w_in)


def _sb_kernel(q_ref, k_ref, v_ref, zg_ref, o_ref, acc_ref, carry_ref):
    seq = q_ref.shape[0]
    T = SB_TILE
    HP = q_ref.shape[1] // DH_B
    rr = lax.broadcasted_iota(jnp.int32, (T, T), 0)
    cc = lax.broadcasted_iota(jnp.int32, (T, T), 1)
    below = cc < rr
    neg_suffix = jnp.where(rr > cc, -1.0, 0.0).astype(BF16)
    lanes = [slice(hh * DH_B, (hh + 1) * DH_B) for hh in range(HP)]

    def tiles(q0, k0, diagonal):
        zs = [_dot_nt(q_ref[pl.ds(q0, T), ln], k_ref[pl.ds(k0, T), ln]) for ln in lanes]
        sps, log_betas = [], []
        for z in zs:
            zb = z.astype(BF16)
            sp = jnp.maximum(zb, 0.0) + jnp.log(1.0 + jnp.exp(-jnp.abs(zb)))
            log_betas.append(zb - sp)
            sps.append(jnp.where(below, sp, 0.0) if diagonal else sp)
        betweens = [_dot(sp, neg_suffix) for sp in sps]
        alive = None
        for hh in range(HP):
            shift = betweens[hh] if diagonal else betweens[hh] + carry_ref[hh]
            a = jnp.exp(log_betas[hh] + shift.astype(BF16))
            if diagonal:
                a = jnp.where(below, a, 0.0)
            av = _dot(a, v_ref[pl.ds(k0, T), lanes[hh]])
            acc_ref[hh] = av if diagonal else acc_ref[hh] + av
            total = betweens[hh][:, 0:1] - sps[hh][:, 0:1].astype(F32)
            carry = total if diagonal else carry_ref[hh] + total
            carry_ref[hh] = carry
            alive = carry if alive is None else jnp.maximum(alive, carry)
        return (jnp.max(alive) > SB_DEAD_LOG).astype(jnp.int32)

    def qblock(qi, c):
        q0 = pl.multiple_of(qi * T, T)
        alive = tiles(q0, q0, True)

        def cond(s):
            return jnp.logical_and(s[0] >= 0, s[1] > 0)

        def body(s):
            return s[0] - 1, tiles(q0, pl.multiple_of(s[0] * T, T), False)

        lax.while_loop(cond, body, (qi - 1, alive))
        for hh in range(HP):
            o_ref[pl.ds(q0, T), lanes[hh]] = (acc_ref[hh] * zg_ref[pl.ds(q0, T), lanes[hh]].astype(F32)).astype(BF16)
        return c

    lax.fori_loop(0, seq // T, qblock, 0)


def _stickbreak(q, k, v, zg, bsz, seq):
    n = bsz * seq
    spec = pl.BlockSpec((seq, SB_HEADS * DH_B), lambda b, h: (b, h))
    return pl.pallas_call(
        _sb_kernel,
        grid=(bsz, NH_B // SB_HEADS),
        in_specs=[spec] * 4,
        out_specs=spec,
        out_shape=jax.ShapeDtypeStruct((n, W_B), BF16),
        scratch_shapes=[pltpu.VMEM((SB_HEADS, SB_TILE, DH_B), F32),
                        pltpu.VMEM((SB_HEADS, SB_TILE, 1), F32)],
        compiler_params=pltpu.CompilerParams(
            dimension_semantics=("arbitrary", "arbitrary"), vmem_limit_bytes=VMEM_LIMIT),
        name="stickbreak",
    )(q, k, v, zg)


def kernel(x, a_w_in, a_gate_b, a_conv_w, a_conv_b, a_head_g, a_w_out, a_ln_g, a_ln_b,
           kv_w, b_w_in, b_w_out, b_ln_g, b_ln_b):
    bsz, seq, d = x.shape
    assert d == D_MODEL and a_w_in.shape[0] == 1 and b_w_in.shape[0] == 1
    n = bsz * seq
    x2 = x.reshape(n, d)

    qk, v, gate, gates = _a_in_proj(x2, jnp.swapaxes(a_w_in, 1, 2), a_gate_b[0].reshape(1, -1),
                                    a_conv_w[0], a_conv_b[0].reshape(1, -1), seq)
    hg = _mlstm(qk, v, gate, gates, a_head_g[0].reshape(1, -1), bsz, seq)
    x1, x1b = _out_ln(hg, x2, a_w_out[0].astype(BF16), a_ln_g[0].reshape(1, -1), a_ln_b[0].reshape(1, -1), True)

    k_sh, v_sh, q, zg = _b_proj(x1b, kv_w, b_w_in)
    att = _stickbreak(q, k_sh, v_sh, zg, bsz, seq)
    (out,) = _out_ln(att, x1, b_w_out[0].astype(BF16), b_ln_g[0].reshape(1, -1), b_ln_b[0].reshape(1, -1), False)
    return out.reshape(bsz, seq, d)
```

```python
import functools

import jax
import jax.numpy as jnp
from jax import lax
from jax.experimental import pallas as pl
from jax.experimental.pallas import tpu as pltpu

F32 = jnp.float32
BF16 = jnp.bfloat16

D_MODEL = 2048
NH_A = 8
DK_A = 128
DV_A = 256
QK_A = NH_A * DK_A
V_A = NH_A * DV_A
CONV_A = 4
NH_B = 16
DH_B = 128
W_B = NH_B * DH_B
DEPTH = 2
ALPHA = (2.0 * DEPTH) ** 0.25
LN_EPS = 1e-5

LANES = 128
VMEM_LIMIT = 56 * 1024 * 1024

MLSTM_CHUNK = 128
MLSTM_HEADS = 4
ROW_SUB = 256
LN_COL_CHUNK = 512
SB_TILE = 256
SB_HEADS = 8
SB_DEAD_LOG = -104.0
NEG_BIG = -1e30


def _dot(a, b):
    return jnp.dot(a, b, preferred_element_type=F32)


def _dot_nt(a, b):
    return lax.dot_general(a, b, (((1,), (1,)), ((), ())), preferred_element_type=F32)


def _dot_tn(a, b):
    return lax.dot_general(a, b, (((0,), (0,)), ((), ())), preferred_element_type=F32)


def _split_bf16(x):
    hi = x.astype(BF16)
    lo = (x - hi.astype(F32)).astype(BF16)
    return hi, lo


def _w_prep_kernel(wt_ref, o_ref):
    o_ref[...] = jnp.transpose(wt_ref[...].astype(BF16))


def _w_prep(w_in_t, n_cols, tr=512):
    return pl.pallas_call(
        _w_prep_kernel,
        grid=(n_cols // tr,),
        in_specs=[pl.BlockSpec((None, tr, D_MODEL), lambda r: (0, r, 0))],
        out_specs=pl.BlockSpec((D_MODEL, tr), lambda r: (0, r)),
        out_shape=jax.ShapeDtypeStruct((D_MODEL, n_cols), BF16),
        compiler_params=pltpu.CompilerParams(dimension_semantics=("arbitrary",), vmem_limit_bytes=VMEM_LIMIT),
        name="w_prep",
    )(w_in_t)


def _a_in_kernel(x_ref, xh_ref, wqk_ref, wv_ref, wo_ref, wz_ref, wg_ref, gb_ref, cw_ref, cb_ref,
                 qk_ref, v_ref, gate_ref, gates_ref, xb_ref, hb_ref, u_ref, *, tiles_per_seq):
    i = pl.program_id(0)
    j = pl.program_id(1)
    tm = x_ref.shape[0]
    halo = xh_ref.shape[0]

    @pl.when(j == 0)
    def _():
        xb = x_ref[...].astype(BF16)
        xb_ref[...] = xb
        keep = (i % tiles_per_seq != 0).astype(F32)
        hb_ref[...] = (xh_ref[...] * keep).astype(BF16)
        g = _dot_nt(xb, wg_ref[...].astype(BF16)) + gb_ref[...]
        lane = lax.broadcasted_iota(jnp.int32, g.shape, 1)
        log_f = jnp.minimum(g, 0.0) - jnp.log1p(jnp.exp(-jnp.abs(g)))
        gates_ref[...] = jnp.where(lane < NH_A, g, log_f)

    wqk = wqk_ref[...]
    wo = wo_ref[...]
    wz = wz_ref[...]
    wv = wv_ref[...]
    w = cw_ref[...]
    bias = cb_ref[...]
    scale = jnp.where(j >= pl.num_programs(1) // 2, DK_A ** -0.5, 1.0)
    u_ref[0:halo, :] = _dot(hb_ref[...], wqk)
    for t in range(tm // ROW_SUB):
        rows = slice(t * ROW_SUB, (t + 1) * ROW_SUB)
        xs = xb_ref[rows, :]
        u_ref[halo + t * ROW_SUB:halo + (t + 1) * ROW_SUB, :] = _dot(xs, wqk)
        o = _dot(xs, wo)
        z = _dot(xs, wz)
        gate_ref[rows, :] = (jax.nn.sigmoid(o) * z * jax.nn.sigmoid(z)).astype(BF16)
        acc = jnp.broadcast_to(bias, (ROW_SUB, bias.shape[1]))
        for kk in range(CONV_A):
            off = halo - (CONV_A - 1) + kk + t * ROW_SUB
            acc = acc + w[kk:kk + 1, :] * u_ref[off:off + ROW_SUB, :]
        qk_ref[rows, :] = (acc * jax.nn.sigmoid(acc) * scale).astype(BF16)
        v_ref[rows, :] = _dot(xs, wv).astype(BF16)


def _a_in_proj(x2, w_main, w_in_t, gate_b, conv_w, conv_b, seq, tm=1024, tn=512):
    n = x2.shape[0]
    nb = D_MODEL // tn
    halo = 16
    n_main = 2 * QK_A + 3 * V_A
    wspec = lambda g: pl.BlockSpec((D_MODEL, tn), lambda i, j, g=g: (0, g * nb + j))
    ospec = pl.BlockSpec((tm, tn), lambda i, j: (i, j))
    return pl.pallas_call(
        functools.partial(_a_in_kernel, tiles_per_seq=seq // tm),
        grid=(n // tm, nb),
        in_specs=[pl.BlockSpec((tm, D_MODEL), lambda i, j: (i, 0)),
                  pl.BlockSpec((halo, D_MODEL), lambda i, j: (jnp.maximum(i * (tm // halo) - 1, 0), 0)),
                  wspec(0), wspec(1), wspec(2), wspec(3),
                  pl.BlockSpec((None, 2 * NH_A, D_MODEL), lambda i, j: (0, n_main // (2 * NH_A), 0)),
                  pl.BlockSpec((1, 2 * NH_A), lambda i, j: (0, 0)),
                  pl.BlockSpec((CONV_A, tn), lambda i, j: (0, j)),
                  pl.BlockSpec((1, tn), lambda i, j: (0, j))],
        out_specs=[ospec, ospec, ospec, pl.BlockSpec((tm, 2 * NH_A), lambda i, j: (i, 0))],
        out_shape=[jax.ShapeDtypeStruct((n, 2 * QK_A), BF16),
                   jax.ShapeDtypeStruct((n, V_A), BF16),
                   jax.ShapeDtypeStruct((n, V_A), BF16),
                   jax.ShapeDtypeStruct((n, 2 * NH_A), F32)],
        scratch_shapes=[pltpu.VMEM((tm, D_MODEL), BF16),
                        pltpu.VMEM((halo, D_MODEL), BF16),
                        pltpu.VMEM((tm + halo, tn), F32)],
        compiler_params=pltpu.CompilerParams(
            dimension_semantics=("arbitrary", "arbitrary"), vmem_limit_bytes=VMEM_LIMIT),
        name="a_in_proj",
    )(x2, x2, w_main, w_main, w_main, w_main, w_in_t, gate_b, conv_w, conv_b)


def _mlstm_kernel(q_s, k_s, v_ref, gate_ref, gates_ref, hg_ref, out_ref, st_ref, m_ref, bc_s, ar_s):
    seq = q_s.shape[0]
    L = MLSTM_CHUNK
    HP = MLSTM_HEADS
    nchunk = seq // L
    head0 = pl.program_id(1) * HP

    st_ref[...] = jnp.zeros(st_ref.shape, F32)
    m_ref[...] = jnp.zeros(m_ref.shape, F32)

    ng = 2 * NH_A
    rr = lax.broadcasted_iota(jnp.int32, (L, L), 0)
    cc = lax.broadcasted_iota(jnp.int32, (L, L), 1)
    causal = cc <= rr
    tri = jnp.where(causal, 1.0, 0.0).astype(BF16)
    tri2 = jnp.concatenate([tri, tri], axis=1)
    ones_v = jnp.ones((L, LANES), BF16)
    heads = range(HP)
    qk_l = [slice(h * DK_A, (h + 1) * DK_A) for h in heads]
    v_l = [slice(h * DV_A, (h + 1) * DV_A) for h in heads]

    g = gates_ref[...]
    g_hi, g_lo = _split_bf16(g)
    cum = jnp.concatenate(
        [_dot(tri2, jnp.concatenate([g_hi[c * L:(c + 1) * L], g_lo[c * L:(c + 1) * L]], axis=0))
         for c in range(nchunk)], axis=0)
    c_hi, c_lo = _split_bf16(cum)
    a_hi, a_lo = _split_bf16(g[:, :NH_A] - cum[:, NH_A:])
    a2 = jnp.concatenate([a_hi, a_lo], axis=1)
    eye_r = lax.broadcasted_iota(jnp.int32, (NH_A, ng), 0)
    eye_c = lax.broadcasted_iota(jnp.int32, (NH_A, ng), 1)
    eye2 = jnp.where(eye_c % NH_A == eye_r, 1.0, 0.0).astype(BF16)
    for c in range(nchunk):
        ar_s[c * NH_A:(c + 1) * NH_A, :] = _dot_nt(eye2, a2[c * L:(c + 1) * L])
    c2 = jnp.concatenate([c_hi[:, NH_A:], c_lo[:, NH_A:]], axis=1)
    sel_r = lax.broadcasted_iota(jnp.int32, (ng, 2 * LANES), 0)
    sel_c = lax.broadcasted_iota(jnp.int32, (ng, 2 * LANES), 1)
    for p in range(HP // 2):
        want = head0 + 2 * p + jnp.where(sel_c < LANES, 0, 1)
        onehot = jnp.where(sel_r % NH_A == want, 1.0, 0.0).astype(BF16)
        pair = _dot(c2, onehot)
        bc_s[2 * p] = pair[:, :LANES]
        bc_s[2 * p + 1] = pair[:, LANES:]

    def chunk(c, carry):
        r0 = pl.multiple_of(c * L, L)
        a0 = c * NH_A
        qs = [q_s[pl.ds(r0, L), qk_l[h]] for h in heads]
        ks = [k_s[pl.ds(r0, L), qk_l[h]] for h in heads]
        k_ts = [jnp.transpose(ks[h]) for h in heads]
        vexts = [jnp.concatenate([v_ref[pl.ds(r0, L), v_l[h]], ones_v], axis=1) for h in heads]
        sts = [st_ref[h] for h in heads]
        m_sts = [m_ref[h] for h in heads]
        bcums = [bc_s[h, pl.ds(r0, L), :] for h in heads]
        a_rows = [ar_s[pl.ds(a0 + head0 + h, 1), :] for h in heads]

        qk = [_dot_nt(qs[h], ks[h]) for h in heads]
        carrieds = [_dot(qs[h], sts[h].astype(BF16)) for h in heads]
        gtots = [bc[L - 1:L, :] for bc in bcums]
        dmats = [jnp.where(causal, bcums[h] + a_rows[h], NEG_BIG) for h in heads]
        inters = [bcums[h] + m_sts[h] for h in heads]
        m_qs = [jnp.maximum(inters[h], jnp.max(dmats[h], axis=1, keepdims=True)) for h in heads]
        scores = [(qk[h] * jnp.exp(dmats[h] - m_qs[h])).astype(BF16) for h in heads]
        intras = [_dot(scores[h], vexts[h]) for h in heads]

        for h in heads:
            wlog = gtots[h] + a_rows[h]
            m_new = jnp.maximum(gtots[h] + m_sts[h], jnp.max(wlog, axis=1, keepdims=True))
            wk_row = jnp.exp(wlog - m_new)
            kw_t = (k_ts[h].astype(F32) * wk_row).astype(BF16)
            decay = jnp.exp(gtots[h] + m_sts[h] - m_new)
            st_ref[h] = jnp.concatenate([decay] * 3, axis=1) * sts[h] + _dot(kw_t, vexts[h])
            m_ref[h] = m_new

        for h in heads:
            w_inter = jnp.exp(inters[h] - m_qs[h])
            tot = intras[h] + jnp.concatenate([w_inter] * 3, axis=1) * carrieds[h]
            inv = 1.0 / jnp.maximum(jnp.abs(tot[:, DV_A:]), jnp.exp(-m_qs[h]))
            hv = tot[:, :DV_A] * jnp.concatenate([inv, inv], axis=1)
            mu = jnp.mean(hv, axis=1, keepdims=True)
            hc = hv - mu
            var = jnp.mean(hc * hc, axis=1, keepdims=True)
            hn = hc * lax.rsqrt(var + LN_EPS)
            gate = gate_ref[pl.ds(r0, L), v_l[h]].astype(F32)
            out_ref[pl.ds(r0, L), v_l[h]] = (hn * hg_ref[:, v_l[h]] * gate).astype(BF16)
        return carry

    lax.fori_loop(0, nchunk, chunk, 0)


def _mlstm(qk, v, gate, gates, head_g, bsz, seq):
    n = bsz * seq
    hp = MLSTM_HEADS
    ngrp = NH_A // hp
    return pl.pallas_call(
        _mlstm_kernel,
        grid=(bsz, ngrp),
        in_specs=[pl.BlockSpec((seq, hp * DK_A), lambda b, h: (b, h)),
                  pl.BlockSpec((seq, hp * DK_A), lambda b, h: (b, ngrp + h)),
                  pl.BlockSpec((seq, hp * DV_A), lambda b, h: (b, h)),
                  pl.BlockSpec((seq, hp * DV_A), lambda b, h: (b, h)),
                  pl.BlockSpec((seq, 2 * NH_A), lambda b, h: (b, 0)),
                  pl.BlockSpec((1, hp * DV_A), lambda b, h: (0, h))],
        out_specs=pl.BlockSpec((seq, hp * DV_A), lambda b, h: (b, h)),
        out_shape=jax.ShapeDtypeStruct((n, V_A), BF16),
        scratch_shapes=[pltpu.VMEM((hp, DK_A, DV_A + LANES), F32),
                        pltpu.VMEM((hp, 1, LANES), F32),
                        pltpu.VMEM((hp, seq, LANES), F32),
                        pltpu.VMEM((NH_A * (seq // MLSTM_CHUNK), MLSTM_CHUNK), F32)],
        compiler_params=pltpu.CompilerParams(
            dimension_semantics=("arbitrary", "arbitrary"), vmem_limit_bytes=VMEM_LIMIT),
        name="mlstm",
    )(qk, qk, v, gate, gates, head_g)


def _out_ln_kernel(h_ref, x_ref, w_ref, g_ref, b_ref, *out_refs):
    tm, d = x_ref.shape
    inv_d = 1.0 / d
    for t in range(tm // ROW_SUB):
        rows = slice(t * ROW_SUB, (t + 1) * ROW_SUB)
        hs = h_ref[rows, :]
        s1 = jnp.zeros((ROW_SUB, LANES), F32)
        s2 = jnp.zeros((ROW_SUB, LANES), F32)
        for c in range(d // LN_COL_CHUNK):
            cols = slice(c * LN_COL_CHUNK, (c + 1) * LN_COL_CHUNK)
            r = ALPHA * x_ref[rows, cols] + _dot(hs, w_ref[:, cols])
            out_refs[0][rows, cols] = r
            for l in range(LN_COL_CHUNK // LANES):
                rl = r[:, l * LANES:(l + 1) * LANES]
                s1 = s1 + rl
                s2 = s2 + rl * rl
        mu = jnp.sum(s1, axis=1, keepdims=True) * inv_d
        var = jnp.sum(s2, axis=1, keepdims=True) * inv_d - mu * mu
        rstd = lax.rsqrt(var + LN_EPS)
        for c in range(d // LN_COL_CHUNK):
            cols = slice(c * LN_COL_CHUNK, (c + 1) * LN_COL_CHUNK)
            out = (out_refs[0][rows, cols] - mu) * rstd * g_ref[:, cols] + b_ref[:, cols]
            out_refs[0][rows, cols] = out
            if len(out_refs) > 1:
                out_refs[1][rows, cols] = out.astype(BF16)


def _out_ln(h, x2, w, ln_g, ln_b, with_bf16, tm=512):
    n = x2.shape[0]
    row = pl.BlockSpec((tm, D_MODEL), lambda i: (i, 0))
    vec = pl.BlockSpec((1, D_MODEL), lambda i: (0, 0))
    out_shape = [jax.ShapeDtypeStruct((n, D_MODEL), F32)]
    out_specs = [row]
    if with_bf16:
        out_shape.append(jax.ShapeDtypeStruct((n, D_MODEL), BF16))
        out_specs.append(row)
    return pl.pallas_call(
        _out_ln_kernel,
        grid=(n // tm,),
        in_specs=[row, row, pl.BlockSpec((D_MODEL, D_MODEL), lambda i: (0, 0)), vec, vec],
        out_specs=out_specs,
        out_shape=out_shape,
        compiler_params=pltpu.CompilerParams(
            dimension_semantics=("arbitrary",), vmem_limit_bytes=VMEM_LIMIT),
        name="out_ln",
    )(h, x2, w, ln_g, ln_b)


def _b_proj_kernel(x_ref, wk_ref, wv_ref, wq_ref, wz_ref, k_ref, v_ref, q_ref, zg_ref):
    wk = wk_ref[...].astype(BF16)
    wv = wv_ref[...].astype(BF16)
    wq = wq_ref[...].astype(BF16)
    wz = wz_ref[...].astype(BF16)
    for t in range(x_ref.shape[0] // ROW_SUB):
        rows = slice(t * ROW_SUB, (t + 1) * ROW_SUB)
        xs = x_ref[rows, :]
        k_ref[rows, :] = _dot(xs, wk).astype(BF16)
        v_ref[rows, :] = _dot(xs, wv).astype(BF16)
        q_ref[rows, :] = (_dot(xs, wq) * (DH_B ** -0.5)).astype(BF16)
        z = _dot(xs, wz)
        zg_ref[rows, :] = (z * jax.nn.sigmoid(z)).astype(BF16)


def _b_proj(x1b, kv_w, b_w_in, tm=2048, tn=256):
    n = x1b.shape[0]
    nb = W_B // tn
    wspec = lambda g: pl.BlockSpec((D_MODEL, tn), lambda i, j, g=g: (0, g * nb + j))
    wspec3 = lambda g: pl.BlockSpec((None, D_MODEL, tn), lambda i, j, g=g: (0, 0, g * nb + j))
    ospec = pl.BlockSpec((tm, tn), lambda i, j: (i, j))
    sds = jax.ShapeDtypeStruct((n, W_B), BF16)
    return pl.pallas_call(
        _b_proj_kernel,
        grid=(n // tm, nb),
        in_specs=[pl.BlockSpec((tm, D_MODEL), lambda i, j: (i, 0)), wspec(0), wspec(1), wspec3(0), wspec3(1)],
        out_specs=[ospec] * 4,
        out_shape=[sds] * 4,
        compiler_params=pltpu.CompilerParams(
            dimension_semantics=("arbitrary", "arbitrary"), vmem_limit_bytes=VMEM_LIMIT),
        name="b_proj",
    )(x1b, kv_w, kv_w, b_w_in, b_w_in)


def _sb_kernel(q_ref, k_ref, v_ref, zg_ref, o_ref, acc_ref, carry_ref):
    seq = q_ref.shape[0]
    T = SB_TILE
    HP = q_ref.shape[1] // DH_B
    rr = lax.broadcasted_iota(jnp.int32, (T, T), 0)
    cc = lax.broadcasted_iota(jnp.int32, (T, T), 1)
    below = cc < rr
    neg_suffix = jnp.where(rr > cc, -1.0, 0.0).astype(BF16)
    lanes = [slice(hh * DH_B, (hh + 1) * DH_B) for hh in range(HP)]

    def tiles(q0, k0, diagonal):
        zs = [_dot_nt(q_ref[pl.ds(q0, T), ln], k_ref[pl.ds(k0, T), ln]) for ln in lanes]
        sps, log_betas = [], []
        for z in zs:
            zb = z.astype(BF16)
            sp = jnp.maximum(zb, 0.0) + jnp.log(1.0 + jnp.exp(-jnp.abs(zb)))
            log_betas.append(zb - sp)
            sps.append(jnp.where(below, sp, 0.0) if diagonal else sp)
        betweens = [_dot(sp, neg_suffix) for sp in sps]
        alive = None
        for hh in range(HP):
            shift = betweens[hh] if diagonal else betweens[hh] + carry_ref[hh]
            a = jnp.exp(log_betas[hh] + shift.astype(BF16))
            if diagonal:
                a = jnp.where(below, a, 0.0)
            av = _dot(a, v_ref[pl.ds(k0, T), lanes[hh]])
            acc_ref[hh] = av if diagonal else acc_ref[hh] + av
            total = betweens[hh][:, 0:1] - sps[hh][:, 0:1].astype(F32)
            carry = total if diagonal else carry_ref[hh] + total
            carry_ref[hh] = carry
            alive = carry if alive is None else jnp.maximum(alive, carry)
        return (jnp.max(alive) > SB_DEAD_LOG).astype(jnp.int32)

    def qblock(qi, c):
        q0 = pl.multiple_of(qi * T, T)
        alive = tiles(q0, q0, True)

        def cond(s):
            return jnp.logical_and(s[0] >= 0, s[1] > 0)

        def body(s):
            return s[0] - 1, tiles(q0, pl.multiple_of(s[0] * T, T), False)

        lax.while_loop(cond, body, (qi - 1, alive))
        for hh in range(HP):
            o_ref[pl.ds(q0, T), lanes[hh]] = (acc_ref[hh] * zg_ref[pl.ds(q0, T), lanes[hh]].astype(F32)).astype(BF16)
        return c

    lax.fori_loop(0, seq // T, qblock, 0)


def _stickbreak(q, k, v, zg, bsz, seq):
    n = bsz * seq
    spec = pl.BlockSpec((seq, SB_HEADS * DH_B), lambda b, h: (b, h))
    return pl.pallas_call(
        _sb_kernel,
        grid=(bsz, NH_B // SB_HEADS),
        in_specs=[spec] * 4,
        out_specs=spec,
        out_shape=jax.ShapeDtypeStruct((n, W_B), BF16),
        scratch_shapes=[pltpu.VMEM((SB_HEADS, SB_TILE, DH_B), F32),
                        pltpu.VMEM((SB_HEADS, SB_TILE, 1), F32)],
        compiler_params=pltpu.CompilerParams(
            dimension_semantics=("arbitrary", "arbitrary"), vmem_limit_bytes=VMEM_LIMIT),
        name="stickbreak",
    )(q, k, v, zg)


def kernel(x, a_w_in, a_gate_b, a_conv_w, a_conv_b, a_head_g, a_w_out, a_ln_g, a_ln_b,
           kv_w, b_w_in, b_w_out, b_ln_g, b_ln_b):
    bsz, seq, d = x.shape
    assert d == D_MODEL and a_w_in.shape[0] == 1 and b_w_in.shape[0] == 1
    n = bsz * seq
    x2 = x.reshape(n, d)

    w_in_t = jnp.swapaxes(a_w_in, 1, 2)
    w_main = _w_prep(w_in_t, 2 * QK_A + 3 * V_A)
    qk, v, gate, gates = _a_in_proj(x2, w_main, w_in_t, a_gate_b[0].reshape(1, -1),
                                    a_conv_w[0], a_conv_b[0].reshape(1, -1), seq)
    hg = _mlstm(qk, v, gate, gates, a_head_g[0].reshape(1, -1), bsz, seq)
    x1, x1b = _out_ln(hg, x2, a_w_out[0].astype(BF16), a_ln_g[0].reshape(1, -1), a_ln_b[0].reshape(1, -1), True)

    k_sh, v_sh, q, zg = _b_proj(x1b, kv_w, b_w_in)
    att = _stickbreak(q, k_sh, v_sh, zg, bsz, seq)
    (out,) = _out_ln(att, x1, b_w_out[0].astype(BF16), b_ln_g[0].reshape(1, -1), b_ln_b[0].reshape(1, -1), False)
    return out.reshape(bsz, seq, d)
```

```python
import functools

import jax
import jax.numpy as jnp
from jax import lax
from jax.experimental import pallas as pl
from jax.experimental.pallas import tpu as pltpu

F32 = jnp.float32
BF16 = jnp.bfloat16

D_MODEL = 2048
NH_A = 8
DK_A = 128
DV_A = 256
QK_A = NH_A * DK_A
V_A = NH_A * DV_A
CONV_A = 4
NH_B = 16
DH_B = 128
W_B = NH_B * DH_B
DEPTH = 2
ALPHA = (2.0 * DEPTH) ** 0.25
LN_EPS = 1e-5

LANES = 128
VMEM_LIMIT = 56 * 1024 * 1024

MLSTM_CHUNK = 128
MLSTM_HEADS = 4
ROW_SUB = 256
LN_COL_CHUNK = 512
SB_TILE = 256
SB_HEADS = 8
SB_DEAD_LOG = -104.0
NEG_BIG = -1e30


def _dot(a, b):
    return jnp.dot(a, b, preferred_element_type=F32)


def _dot_nt(a, b):
    return lax.dot_general(a, b, (((1,), (1,)), ((), ())), preferred_element_type=F32)


def _split_bf16(x):
    hi = x.astype(BF16)
    lo = (x - hi.astype(F32)).astype(BF16)
    return hi, lo


def _w_prep_kernel(wt_ref, o_ref):
    o_ref[...] = jnp.transpose(wt_ref[...].astype(BF16))


def _w_prep(w_in_t, n_cols, tr=1024):
    return pl.pallas_call(
        _w_prep_kernel,
        grid=(n_cols // tr,),
        in_specs=[pl.BlockSpec((None, tr, D_MODEL), lambda r: (0, r, 0))],
        out_specs=pl.BlockSpec((D_MODEL, tr), lambda r: (0, r)),
        out_shape=jax.ShapeDtypeStruct((D_MODEL, n_cols), BF16),
        compiler_params=pltpu.CompilerParams(dimension_semantics=("arbitrary",), vmem_limit_bytes=VMEM_LIMIT),
        name="w_prep",
    )(w_in_t)


def _a_in_kernel(x_ref, xh_ref, wqk_ref, wv_ref, wo_ref, wz_ref, wg_ref, gb_ref, cw_ref, cb_ref,
                 qk_ref, v_ref, gate_ref, gates_ref, xb_ref, hb_ref, u_ref, *, tiles_per_seq):
    i = pl.program_id(0)
    j = pl.program_id(1)
    tm = x_ref.shape[0]
    halo = xh_ref.shape[0]

    @pl.when(j == 0)
    def _():
        xb = x_ref[...].astype(BF16)
        xb_ref[...] = xb
        keep = (i % tiles_per_seq != 0).astype(F32)
        hb_ref[...] = (xh_ref[...] * keep).astype(BF16)
        g = _dot_nt(xb, wg_ref[...].astype(BF16)) + gb_ref[...]
        lane = lax.broadcasted_iota(jnp.int32, g.shape, 1)
        log_f = jnp.minimum(g, 0.0) - jnp.log1p(jnp.exp(-jnp.abs(g)))
        gates_ref[...] = jnp.where(lane < NH_A, g, log_f)

    wqk = wqk_ref[...]
    wo = wo_ref[...]
    wz = wz_ref[...]
    wv = wv_ref[...]
    w = cw_ref[...]
    bias = cb_ref[...]
    scale = jnp.where(j >= pl.num_programs(1) // 2, DK_A ** -0.5, 1.0)
    u_ref[0:halo, :] = _dot(hb_ref[...], wqk)
    for t in range(tm // ROW_SUB):
        rows = slice(t * ROW_SUB, (t + 1) * ROW_SUB)
        xs = xb_ref[rows, :]
        u_ref[halo + t * ROW_SUB:halo + (t + 1) * ROW_SUB, :] = _dot(xs, wqk)
        o = _dot(xs, wo)
        z = _dot(xs, wz)
        gate_ref[rows, :] = (jax.nn.sigmoid(o) * z * jax.nn.sigmoid(z)).astype(BF16)
        acc = jnp.broadcast_to(bias, (ROW_SUB, bias.shape[1]))
        for kk in range(CONV_A):
            off = halo - (CONV_A - 1) + kk + t * ROW_SUB
            acc = acc + w[kk:kk + 1, :] * u_ref[off:off + ROW_SUB, :]
        qk_ref[rows, :] = (acc * jax.nn.sigmoid(acc) * scale).astype(BF16)
        v_ref[rows, :] = _dot(xs, wv).astype(BF16)


def _a_in_proj(x2, w_main, w_in_t, gate_b, conv_w, conv_b, seq, tm=1024, tn=512):
    n = x2.shape[0]
    nb = D_MODEL // tn
    halo = 16
    n_main = 2 * QK_A + 3 * V_A
    wspec = lambda g: pl.BlockSpec((D_MODEL, tn), lambda i, j, g=g: (0, g * nb + j))
    ospec = pl.BlockSpec((tm, tn), lambda i, j: (i, j))
    return pl.pallas_call(
        functools.partial(_a_in_kernel, tiles_per_seq=seq // tm),
        grid=(n // tm, nb),
        in_specs=[pl.BlockSpec((tm, D_MODEL), lambda i, j: (i, 0)),
                  pl.BlockSpec((halo, D_MODEL), lambda i, j: (jnp.maximum(i * (tm // halo) - 1, 0), 0)),
                  wspec(0), wspec(1), wspec(2), wspec(3),
                  pl.BlockSpec((None, 2 * NH_A, D_MODEL), lambda i, j: (0, n_main // (2 * NH_A), 0)),
                  pl.BlockSpec((1, 2 * NH_A), lambda i, j: (0, 0)),
                  pl.BlockSpec((CONV_A, tn), lambda i, j: (0, j)),
                  pl.BlockSpec((1, tn), lambda i, j: (0, j))],
        out_specs=[ospec, ospec, ospec, pl.BlockSpec((tm, 2 * NH_A), lambda i, j: (i, 0))],
        out_shape=[jax.ShapeDtypeStruct((n, 2 * QK_A), BF16),
                   jax.ShapeDtypeStruct((n, V_A), BF16),
                   jax.ShapeDtypeStruct((n, V_A), BF16),
                   jax.ShapeDtypeStruct((n, 2 * NH_A), F32)],
        scratch_shapes=[pltpu.VMEM((tm, D_MODEL), BF16),
                        pltpu.VMEM((halo, D_MODEL), BF16),
                        pltpu.VMEM((tm + halo, tn), F32)],
        compiler_params=pltpu.CompilerParams(
            dimension_semantics=("arbitrary", "arbitrary"), vmem_limit_bytes=VMEM_LIMIT),
        name="a_in_proj",
    )(x2, x2, w_main, w_main, w_main, w_main, w_in_t, gate_b, conv_w, conv_b)


def _mlstm_kernel(q_s, k_s, v_ref, gate_ref, gates_ref, hg_ref, out_ref, st_ref, m_ref, bc_s, ar_s):
    seq = q_s.shape[0]
    L = MLSTM_CHUNK
    HP = MLSTM_HEADS
    nchunk = seq // L
    head0 = pl.program_id(1) * HP

    st_ref[...] = jnp.zeros(st_ref.shape, F32)
    m_ref[...] = jnp.zeros(m_ref.shape, F32)

    ng = 2 * NH_A
    rr = lax.broadcasted_iota(jnp.int32, (L, L), 0)
    cc = lax.broadcasted_iota(jnp.int32, (L, L), 1)
    causal = cc <= rr
    tri = jnp.where(causal, 1.0, 0.0).astype(BF16)
    tri2 = jnp.concatenate([tri, tri], axis=1)
    ones_v = jnp.ones((L, LANES), BF16)
    heads = range(HP)
    qk_l = [slice(h * DK_A, (h + 1) * DK_A) for h in heads]
    v_l = [slice(h * DV_A, (h + 1) * DV_A) for h in heads]

    g = gates_ref[...]
    g_hi, g_lo = _split_bf16(g)
    cum = jnp.concatenate(
        [_dot(tri2, jnp.concatenate([g_hi[c * L:(c + 1) * L], g_lo[c * L:(c + 1) * L]], axis=0))
         for c in range(nchunk)], axis=0)
    c_hi, c_lo = _split_bf16(cum)
    a_hi, a_lo = _split_bf16(g[:, :NH_A] - cum[:, NH_A:])
    a2 = jnp.concatenate([a_hi, a_lo], axis=1)
    eye_r = lax.broadcasted_iota(jnp.int32, (NH_A, ng), 0)
    eye_c = lax.broadcasted_iota(jnp.int32, (NH_A, ng), 1)
    eye2 = jnp.where(eye_c % NH_A == eye_r, 1.0, 0.0).astype(BF16)
    for c in range(nchunk):
        ar_s[c * NH_A:(c + 1) * NH_A, :] = _dot_nt(eye2, a2[c * L:(c + 1) * L])
    c2 = jnp.concatenate([c_hi[:, NH_A:], c_lo[:, NH_A:]], axis=1)
    sel_r = lax.broadcasted_iota(jnp.int32, (ng, 2 * LANES), 0)
    sel_c = lax.broadcasted_iota(jnp.int32, (ng, 2 * LANES), 1)
    for p in range(HP // 2):
        want = head0 + 2 * p + jnp.where(sel_c < LANES, 0, 1)
        onehot = jnp.where(sel_r % NH_A == want, 1.0, 0.0).astype(BF16)
        pair = _dot(c2, onehot)
        bc_s[2 * p] = pair[:, :LANES]
        bc_s[2 * p + 1] = pair[:, LANES:]

    def chunk(c, carry):
        r0 = pl.multiple_of(c * L, L)
        a0 = c * NH_A
        qs = [q_s[pl.ds(r0, L), qk_l[h]] for h in heads]
        ks = [k_s[pl.ds(r0, L), qk_l[h]] for h in heads]
        k_ts = [jnp.transpose(ks[h]) for h in heads]
        vexts = [jnp.concatenate([v_ref[pl.ds(r0, L), v_l[h]], ones_v], axis=1) for h in heads]
        sts = [st_ref[h] for h in heads]
        m_sts = [m_ref[h] for h in heads]
        bcums = [bc_s[h, pl.ds(r0, L), :] for h in heads]
        a_rows = [ar_s[pl.ds(a0 + head0 + h, 1), :] for h in heads]

        qk = [_dot_nt(qs[h], ks[h]) for h in heads]
        carrieds = [_dot(qs[h], sts[h].astype(BF16)) for h in heads]
        gtots = [bc[L - 1:L, :] for bc in bcums]
        dmats = [jnp.where(causal, bcums[h] + a_rows[h], NEG_BIG) for h in heads]
        inters = [bcums[h] + m_sts[h] for h in heads]
        m_qs = [jnp.maximum(inters[h], jnp.max(dmats[h], axis=1, keepdims=True)) for h in heads]
        scores = [(qk[h] * jnp.exp(dmats[h] - m_qs[h])).astype(BF16) for h in heads]
        intras = [_dot(scores[h], vexts[h]) for h in heads]

        for h in heads:
            wlog = gtots[h] + a_rows[h]
            m_new = jnp.maximum(gtots[h] + m_sts[h], jnp.max(wlog, axis=1, keepdims=True))
            wk_row = jnp.exp(wlog - m_new)
            kw_t = (k_ts[h].astype(F32) * wk_row).astype(BF16)
            decay = jnp.exp(gtots[h] + m_sts[h] - m_new)
            st_ref[h] = jnp.concatenate([decay] * 3, axis=1) * sts[h] + _dot(kw_t, vexts[h])
            m_ref[h] = m_new

        for h in heads:
            w_inter = jnp.exp(inters[h] - m_qs[h])
            tot = intras[h] + jnp.concatenate([w_inter] * 3, axis=1) * carrieds[h]
            inv = 1.0 / jnp.maximum(jnp.abs(tot[:, DV_A:]), jnp.exp(-m_qs[h]))
            hv = tot[:, :DV_A] * jnp.concatenate([inv, inv], axis=1)
            mu = jnp.mean(hv, axis=1, keepdims=True)
            hc = hv - mu
            var = jnp.mean(hc * hc, axis=1, keepdims=True)
            hn = hc * lax.rsqrt(var + LN_EPS)
            gate = gate_ref[pl.ds(r0, L), v_l[h]].astype(F32)
            out_ref[pl.ds(r0, L), v_l[h]] = (hn * hg_ref[:, v_l[h]] * gate).astype(BF16)
        return carry

    lax.fori_loop(0, nchunk, chunk, 0)


def _mlstm(qk, v, gate, gates, head_g, bsz, seq):
    n = bsz * seq
    hp = MLSTM_HEADS
    ngrp = NH_A // hp
    return pl.pallas_call(
        _mlstm_kernel,
        grid=(bsz, ngrp),
        in_specs=[pl.BlockSpec((seq, hp * DK_A), lambda b, h: (b, h)),
                  pl.BlockSpec((seq, hp * DK_A), lambda b, h: (b, ngrp + h)),
                  pl.BlockSpec((seq, hp * DV_A), lambda b, h: (b, h)),
                  pl.BlockSpec((seq, hp * DV_A), lambda b, h: (b, h)),
                  pl.BlockSpec((seq, 2 * NH_A), lambda b, h: (b, 0)),
                  pl.BlockSpec((1, hp * DV_A), lambda b, h: (0, h))],
        out_specs=pl.BlockSpec((seq, hp * DV_A), lambda b, h: (b, h)),
        out_shape=jax.ShapeDtypeStruct((n, V_A), BF16),
        scratch_shapes=[pltpu.VMEM((hp, DK_A, DV_A + LANES), F32),
                        pltpu.VMEM((hp, 1, LANES), F32),
                        pltpu.VMEM((hp, seq, LANES), F32),
                        pltpu.VMEM((NH_A * (seq // MLSTM_CHUNK), MLSTM_CHUNK), F32)],
        compiler_params=pltpu.CompilerParams(
            dimension_semantics=("arbitrary", "arbitrary"), vmem_limit_bytes=VMEM_LIMIT),
        name="mlstm",
    )(qk, qk, v, gate, gates, head_g)


def _out_ln_kernel(h_ref, x_ref, w_ref, g_ref, b_ref, *out_refs):
    tm, d = x_ref.shape
    inv_d = 1.0 / d
    for t in range(tm // ROW_SUB):
        rows = slice(t * ROW_SUB, (t + 1) * ROW_SUB)
        hs = h_ref[rows, :]
        s1 = jnp.zeros((ROW_SUB, LANES), F32)
        s2 = jnp.zeros((ROW_SUB, LANES), F32)
        for c in range(d // LN_COL_CHUNK):
            cols = slice(c * LN_COL_CHUNK, (c + 1) * LN_COL_CHUNK)
            r = ALPHA * x_ref[rows, cols] + _dot(hs, w_ref[:, cols])
            out_refs[0][rows, cols] = r
            for l in range(LN_COL_CHUNK // LANES):
                rl = r[:, l * LANES:(l + 1) * LANES]
                s1 = s1 + rl
                s2 = s2 + rl * rl
        mu = jnp.sum(s1, axis=1, keepdims=True) * inv_d
        var = jnp.sum(s2, axis=1, keepdims=True) * inv_d - mu * mu
        rstd = lax.rsqrt(var + LN_EPS)
        for c in range(d // LN_COL_CHUNK):
            cols = slice(c * LN_COL_CHUNK, (c + 1) * LN_COL_CHUNK)
            out = (out_refs[0][rows, cols] - mu) * rstd * g_ref[:, cols] + b_ref[:, cols]
            out_refs[0][rows, cols] = out
            if len(out_refs) > 1:
                out_refs[1][rows, cols] = out.astype(BF16)


def _out_ln(h, x2, w, ln_g, ln_b, with_bf16, tm=512):
    n = x2.shape[0]
    row = pl.BlockSpec((tm, D_MODEL), lambda i: (i, 0))
    vec = pl.BlockSpec((1, D_MODEL), lambda i: (0, 0))
    out_shape = [jax.ShapeDtypeStruct((n, D_MODEL), F32)]
    out_specs = [row]
    if with_bf16:
        out_shape.append(jax.ShapeDtypeStruct((n, D_MODEL), BF16))
        out_specs.append(row)
    return pl.pallas_call(
        _out_ln_kernel,
        grid=(n // tm,),
        in_specs=[row, row, pl.BlockSpec((D_MODEL, D_MODEL), lambda i: (0, 0)), vec, vec],
        out_specs=out_specs,
        out_shape=out_shape,
        compiler_params=pltpu.CompilerParams(
            dimension_semantics=("arbitrary",), vmem_limit_bytes=VMEM_LIMIT),
        name="out_ln",
    )(h, x2, w, ln_g, ln_b)


def _b_proj_kernel(x_ref, wk_ref, wv_ref, wq_ref, wz_ref, k_ref, v_ref, q_ref, zg_ref):
    wk = wk_ref[...].astype(BF16)
    wv = wv_ref[...].astype(BF16)
    wq = wq_ref[...].astype(BF16)
    wz = wz_ref[...].astype(BF16)
    for t in range(x_ref.shape[0] // ROW_SUB):
        rows = slice(t * ROW_SUB, (t + 1) * ROW_SUB)
        xs = x_ref[rows, :]
        k_ref[rows, :] = _dot(xs, wk).astype(BF16)
        v_ref[rows, :] = _dot(xs, wv).astype(BF16)
        q_ref[rows, :] = (_dot(xs, wq) * (DH_B ** -0.5)).astype(BF16)
        z = _dot(xs, wz)
        zg_ref[rows, :] = (z * jax.nn.sigmoid(z)).astype(BF16)


def _b_proj(x1b, kv_w, b_w_in, tm=2048, tn=256):
    n = x1b.shape[0]
    nb = W_B // tn
    wspec = lambda g: pl.BlockSpec((D_MODEL, tn), lambda i, j, g=g: (0, g * nb + j))
    wspec3 = lambda g: pl.BlockSpec((None, D_MODEL, tn), lambda i, j, g=g: (0, 0, g * nb + j))
    ospec = pl.BlockSpec((tm, tn), lambda i, j: (i, j))
    sds = jax.ShapeDtypeStruct((n, W_B), BF16)
    return pl.pallas_call(
        _b_proj_kernel,
        grid=(n // tm, nb),
        in_specs=[pl.BlockSpec((tm, D_MODEL), lambda i, j: (i, 0)), wspec(0), wspec(1), wspec3(0), wspec3(1)],
        out_specs=[ospec] * 4,
        out_shape=[sds] * 4,
        compiler_params=pltpu.CompilerParams(
            dimension_semantics=("arbitrary", "arbitrary"), vmem_limit_bytes=VMEM_LIMIT),
        name="b_proj",
    )(x1b, kv_w, kv_w, b_w_in, b_w_in)


def _sb_kernel(q_ref, k_ref, v_ref, zg_ref, o_ref, acc_ref, carry_ref):
    seq = q_ref.shape[0]
    T = SB_TILE
    HP = q_ref.shape[1] // DH_B
    rr = lax.broadcasted_iota(jnp.int32, (T, T), 0)
    cc = lax.broadcasted_iota(jnp.int32, (T, T), 1)
    below = cc < rr
    neg_suffix = jnp.where(rr > cc, -1.0, 0.0).astype(BF16)
    lanes = [slice(hh * DH_B, (hh + 1) * DH_B) for hh in range(HP)]

    def tiles(q0, k0, diagonal):
        zs = [_dot_nt(q_ref[pl.ds(q0, T), ln], k_ref[pl.ds(k0, T), ln]) for ln in lanes]
        sps, log_betas = [], []
        for z in zs:
            zb = z.astype(BF16)
            sp = jnp.maximum(zb, 0.0) + jnp.log(1.0 + jnp.exp(-jnp.abs(zb)))
            log_betas.append(zb - sp)
            sps.append(jnp.where(below, sp, 0.0) if diagonal else sp)
        betweens = [_dot(sp, neg_suffix) for sp in sps]
        alive = None
        for hh in range(HP):
            shift = betweens[hh] if diagonal else betweens[hh] + carry_ref[hh]
            a = jnp.exp(log_betas[hh] + shift.astype(BF16))
            if diagonal:
                a = jnp.where(below, a, 0.0)
            av = _dot(a, v_ref[pl.ds(k0, T), lanes[hh]])
            acc_ref[hh] = av if diagonal else acc_ref[hh] + av
            total = betweens[hh][:, 0:1] - sps[hh][:, 0:1].astype(F32)
            carry = total if diagonal else carry_ref[hh] + total
            carry_ref[hh] = carry
            alive = carry if alive is None else jnp.maximum(alive, carry)
        return (jnp.max(alive) > SB_DEAD_LOG).astype(jnp.int32)

    def qblock(qi, c):
        q0 = pl.multiple_of(qi * T, T)
        alive = tiles(q0, q0, True)

        def cond(s):
            return jnp.logical_and(s[0] >= 0, s[1] > 0)

        def body(s):
            return s[0] - 1, tiles(q0, pl.multiple_of(s[0] * T, T), False)

        lax.while_loop(cond, body, (qi - 1, alive))
        for hh in range(HP):
            o_ref[pl.ds(q0, T), lanes[hh]] = (acc_ref[hh] * zg_ref[pl.ds(q0, T), lanes[hh]].astype(F32)).astype(BF16)
        return c

    lax.fori_loop(0, seq // T, qblock, 0)


def _stickbreak(q, k, v, zg, bsz, seq):
    n = bsz * seq
    spec = pl.BlockSpec((seq, SB_HEADS * DH_B), lambda b, h: (b, h))
    return pl.pallas_call(
        _sb_kernel,
        grid=(bsz, NH_B // SB_HEADS),
        in_specs=[spec] * 4,
        out_specs=spec,
        out_shape=jax.ShapeDtypeStruct((n, W_B), BF16),
        scratch_shapes=[pltpu.VMEM((SB_HEADS, SB_TILE, DH_B), F32),
                        pltpu.VMEM((SB_HEADS, SB_TILE, 1), F32)],
        compiler_params=pltpu.CompilerParams(
            dimension_semantics=("arbitrary", "arbitrary"), vmem_limit_bytes=VMEM_LIMIT),
        name="stickbreak",
    )(q, k, v, zg)


def kernel(x, a_w_in, a_gate_b, a_conv_w, a_conv_b, a_head_g, a_w_out, a_ln_g, a_ln_b,
           kv_w, b_w_in, b_w_out, b_ln_g, b_ln_b):
    bsz, seq, d = x.shape
    assert d == D_MODEL and a_w_in.shape[0] == 1 and b_w_in.shape[0] == 1
    n = bsz * seq
    x2 = x.reshape(n, d)

    w_in_t = jnp.swapaxes(a_w_in, 1, 2)
    w_main = _w_prep(w_in_t, 2 * QK_A + 3 * V_A)
    qk, v, gate, gates = _a_in_proj(x2, w_main, w_in_t, a_gate_b[0].reshape(1, -1),
                                    a_conv_w[0], a_conv_b[0].reshape(1, -1), seq)
    hg = _mlstm(qk, v, gate, gates, a_head_g[0].reshape(1, -1), bsz, seq)
    x1, x1b = _out_ln(hg, x2, a_w_out[0].astype(BF16), a_ln_g[0].reshape(1, -1), a_ln_b[0].reshape(1, -1), True)

    k_sh, v_sh, q, zg = _b_proj(x1b, kv_w, b_w_in)
    att = _stickbreak(q, k_sh, v_sh, zg, bsz, seq)
    (out,) = _out_ln(att, x1, b_w_out[0].astype(BF16), b_ln_g[0].reshape(1, -1), b_ln_b[0].reshape(1, -1), False)
    return out.reshape(bsz, seq, d)
```

```python
import functools

import jax
import jax.numpy as jnp
from jax import lax
from jax.experimental import pallas as pl
from jax.experimental.pallas import tpu as pltpu

F32 = jnp.float32
BF16 = jnp.bfloat16

D_MODEL = 2048
NH_A = 8
DK_A = 128
DV_A = 256
QK_A = NH_A * DK_A
V_A = NH_A * DV_A
CONV_A = 4
NH_B = 16
DH_B = 128
W_B = NH_B * DH_B
DEPTH = 2
ALPHA = (2.0 * DEPTH) ** 0.25
LN_EPS = 1e-5

LANES = 128
VMEM_LIMIT = 56 * 1024 * 1024

MLSTM_CHUNK = 128
MLSTM_HEADS = 4
ROW_SUB = 256
LN_COL_CHUNK = 512
SB_TILE = 256
SB_HEADS = 8
SB_DEAD_LOG = -104.0
NEG_BIG = -1e30


def _dot(a, b):
    return jnp.dot(a, b, preferred_element_type=F32)


def _dot_nt(a, b):
    return lax.dot_general(a, b, (((1,), (1,)), ((), ())), preferred_element_type=F32)


def _split_bf16(x):
    hi = x.astype(BF16)
    lo = (x - hi.astype(F32)).astype(BF16)
    return hi, lo


def _w_prep_kernel(wt_ref, o_ref):
    o_ref[...] = jnp.transpose(wt_ref[...].astype(BF16))


def _w_prep(w_in_t, n_cols, tr=1024):
    return pl.pallas_call(
        _w_prep_kernel,
        grid=(n_cols // tr,),
        in_specs=[pl.BlockSpec((None, tr, D_MODEL), lambda r: (0, r, 0))],
        out_specs=pl.BlockSpec((D_MODEL, tr), lambda r: (0, r)),
        out_shape=jax.ShapeDtypeStruct((D_MODEL, n_cols), BF16),
        compiler_params=pltpu.CompilerParams(dimension_semantics=("arbitrary",), vmem_limit_bytes=VMEM_LIMIT),
        name="w_prep",
    )(w_in_t)


def _a_in_kernel(x_ref, xh_ref, wqk_ref, wv_ref, wo_ref, wz_ref, wg_ref, gb_ref, cw_ref, cb_ref,
                 qk_ref, v_ref, gate_ref, gates_ref, xb_ref, hb_ref, u_ref, *, tiles_per_seq):
    i = pl.program_id(0)
    j = pl.program_id(1)
    tm = x_ref.shape[0]
    halo = xh_ref.shape[0]

    @pl.when(j == 0)
    def _():
        xb = x_ref[...].astype(BF16)
        xb_ref[...] = xb
        keep = (i % tiles_per_seq != 0).astype(F32)
        hb_ref[...] = (xh_ref[...] * keep).astype(BF16)
        g = _dot_nt(xb, wg_ref[...].astype(BF16)) + gb_ref[...]
        lane = lax.broadcasted_iota(jnp.int32, g.shape, 1)
        log_f = jnp.minimum(g, 0.0) - jnp.log1p(jnp.exp(-jnp.abs(g)))
        gates_ref[...] = jnp.where(lane < NH_A, g, log_f)

    wqk = wqk_ref[...]
    wo = wo_ref[...]
    wz = wz_ref[...]
    wv = wv_ref[...]
    w = cw_ref[...]
    bias = cb_ref[...]
    scale = jnp.where(j >= pl.num_programs(1) // 2, DK_A ** -0.5, 1.0)
    u_ref[0:halo, :] = _dot(hb_ref[...], wqk)
    for t in range(tm // ROW_SUB):
        rows = slice(t * ROW_SUB, (t + 1) * ROW_SUB)
        xs = xb_ref[rows, :]
        u_ref[halo + t * ROW_SUB:halo + (t + 1) * ROW_SUB, :] = _dot(xs, wqk)
        o = _dot(xs, wo)
        z = _dot(xs, wz)
        gate_ref[rows, :] = (jax.nn.sigmoid(o) * z * jax.nn.sigmoid(z)).astype(BF16)
        acc = jnp.broadcast_to(bias, (ROW_SUB, bias.shape[1]))
        for kk in range(CONV_A):
            off = halo - (CONV_A - 1) + kk + t * ROW_SUB
            acc = acc + w[kk:kk + 1, :] * u_ref[off:off + ROW_SUB, :]
        qk_ref[rows, :] = (acc * jax.nn.sigmoid(acc) * scale).astype(BF16)
        v_ref[rows, :] = _dot(xs, wv).astype(BF16)


def _a_in_proj(x2, w_main, w_in_t, gate_b, conv_w, conv_b, seq, tm=1024, tn=512):
    n = x2.shape[0]
    nb = D_MODEL // tn
    halo = 16
    n_main = 2 * QK_A + 3 * V_A
    wspec = lambda g: pl.BlockSpec((D_MODEL, tn), lambda i, j, g=g: (0, g * nb + j))
    ospec = pl.BlockSpec((tm, tn), lambda i, j: (i, j))
    return pl.pallas_call(
        functools.partial(_a_in_kernel, tiles_per_seq=seq // tm),
        grid=(n // tm, nb),
        in_specs=[pl.BlockSpec((tm, D_MODEL), lambda i, j: (i, 0)),
                  pl.BlockSpec((halo, D_MODEL), lambda i, j: (jnp.maximum(i * (tm // halo) - 1, 0), 0)),
                  wspec(0), wspec(1), wspec(2), wspec(3),
                  pl.BlockSpec((None, 2 * NH_A, D_MODEL), lambda i, j: (0, n_main // (2 * NH_A), 0)),
                  pl.BlockSpec((1, 2 * NH_A), lambda i, j: (0, 0)),
                  pl.BlockSpec((CONV_A, tn), lambda i, j: (0, j)),
                  pl.BlockSpec((1, tn), lambda i, j: (0, j))],
        out_specs=[ospec, ospec, ospec, pl.BlockSpec((tm, 2 * NH_A), lambda i, j: (i, 0))],
        out_shape=[jax.ShapeDtypeStruct((n, 2 * QK_A), BF16),
                   jax.ShapeDtypeStruct((n, V_A), BF16),
                   jax.ShapeDtypeStruct((n, V_A), BF16),
                   jax.ShapeDtypeStruct((n, 2 * NH_A), F32)],
        scratch_shapes=[pltpu.VMEM((tm, D_MODEL), BF16),
                        pltpu.VMEM((halo, D_MODEL), BF16),
                        pltpu.VMEM((tm + halo, tn), F32)],
        compiler_params=pltpu.CompilerParams(
            dimension_semantics=("arbitrary", "arbitrary"), vmem_limit_bytes=VMEM_LIMIT),
        name="a_in_proj",
    )(x2, x2, w_main, w_main, w_main, w_main, w_in_t, gate_b, conv_w, conv_b)


def _mlstm_kernel(q_s, k_s, v_ref, gate_ref, gates_ref, hg_ref, out_ref, st_ref, m_ref, bc_s, ar_s):
    seq = q_s.shape[0]
    L = MLSTM_CHUNK
    HP = MLSTM_HEADS
    nchunk = seq // L
    head0 = pl.program_id(1) * HP

    st_ref[...] = jnp.zeros(st_ref.shape, F32)
    m_ref[...] = jnp.zeros(m_ref.shape, F32)

    ng = 2 * NH_A
    rr = lax.broadcasted_iota(jnp.int32, (L, L), 0)
    cc = lax.broadcasted_iota(jnp.int32, (L, L), 1)
    causal = cc <= rr
    tri = jnp.where(causal, 1.0, 0.0).astype(BF16)
    tri2 = jnp.concatenate([tri, tri], axis=1)
    ones_v = jnp.ones((L, LANES), BF16)
    heads = range(HP)
    qk_l = [slice(h * DK_A, (h + 1) * DK_A) for h in heads]
    v_l = [slice(h * DV_A, (h + 1) * DV_A) for h in heads]

    g = gates_ref[...]
    g_hi, g_lo = _split_bf16(g)
    cum = jnp.concatenate(
        [_dot(tri2, jnp.concatenate([g_hi[c * L:(c + 1) * L], g_lo[c * L:(c + 1) * L]], axis=0))
         for c in range(nchunk)], axis=0)
    c_hi, c_lo = _split_bf16(cum)
    a_hi, a_lo = _split_bf16(g[:, :NH_A] - cum[:, NH_A:])
    a2 = jnp.concatenate([a_hi, a_lo], axis=1)
    eye_r = lax.broadcasted_iota(jnp.int32, (NH_A, ng), 0)
    eye_c = lax.broadcasted_iota(jnp.int32, (NH_A, ng), 1)
    eye2 = jnp.where(eye_c % NH_A == eye_r, 1.0, 0.0).astype(BF16)
    for c in range(nchunk):
        ar_s[c * NH_A:(c + 1) * NH_A, :] = _dot_nt(eye2, a2[c * L:(c + 1) * L])
    c2 = jnp.concatenate([c_hi[:, NH_A:], c_lo[:, NH_A:]], axis=1)
    sel_r = lax.broadcasted_iota(jnp.int32, (ng, 2 * LANES), 0)
    sel_c = lax.broadcasted_iota(jnp.int32, (ng, 2 * LANES), 1)
    for p in range(HP // 2):
        want = head0 + 2 * p + jnp.where(sel_c < LANES, 0, 1)
        onehot = jnp.where(sel_r % NH_A == want, 1.0, 0.0).astype(BF16)
        pair = _dot(c2, onehot)
        bc_s[2 * p] = pair[:, :LANES]
        bc_s[2 * p + 1] = pair[:, LANES:]

    def chunk(c, carry):
        r0 = pl.multiple_of(c * L, L)
        a0 = c * NH_A
        qs = [q_s[pl.ds(r0, L), qk_l[h]] for h in heads]
        ks = [k_s[pl.ds(r0, L), qk_l[h]] for h in heads]
        k_ts = [jnp.transpose(ks[h]) for h in heads]
        vexts = [jnp.concatenate([v_ref[pl.ds(r0, L), v_l[h]], ones_v], axis=1) for h in heads]
        sts = [st_ref[h] for h in heads]
        m_sts = [m_ref[h] for h in heads]
        bcums = [bc_s[h, pl.ds(r0, L), :] for h in heads]
        a_rows = [ar_s[pl.ds(a0 + head0 + h, 1), :] for h in heads]

        qk = [_dot_nt(qs[h], ks[h]) for h in heads]
        carrieds = [_dot(qs[h], sts[h].astype(BF16)) for h in heads]
        gtots = [bc[L - 1:L, :] for bc in bcums]
        dmats = [jnp.where(causal, bcums[h] + a_rows[h], NEG_BIG) for h in heads]
        inters = [bcums[h] + m_sts[h] for h in heads]
        m_qs = [jnp.maximum(inters[h], jnp.max(dmats[h], axis=1, keepdims=True)) for h in heads]
        scores = [(qk[h] * jnp.exp(dmats[h] - m_qs[h])).astype(BF16) for h in heads]
        intras = [_dot(scores[h], vexts[h]) for h in heads]

        for h in heads:
            wlog = gtots[h] + a_rows[h]
            m_new = jnp.maximum(gtots[h] + m_sts[h], jnp.max(wlog, axis=1, keepdims=True))
            wk_row = jnp.exp(wlog - m_new)
            kw_t = (k_ts[h].astype(F32) * wk_row).astype(BF16)
            decay = jnp.exp(gtots[h] + m_sts[h] - m_new)
            st_ref[h] = jnp.concatenate([decay] * 3, axis=1) * sts[h] + _dot(kw_t, vexts[h])
            m_ref[h] = m_new

        for h in heads:
            w_inter = jnp.exp(inters[h] - m_qs[h])
            tot = intras[h] + jnp.concatenate([w_inter] * 3, axis=1) * carrieds[h]
            inv = 1.0 / jnp.maximum(jnp.abs(tot[:, DV_A:]), jnp.exp(-m_qs[h]))
            hv = tot[:, :DV_A] * jnp.concatenate([inv, inv], axis=1)
            mu = jnp.mean(hv, axis=1, keepdims=True)
            hc = hv - mu
            var = jnp.mean(hc * hc, axis=1, keepdims=True)
            hn = hc * lax.rsqrt(var + LN_EPS)
            gate = gate_ref[pl.ds(r0, L), v_l[h]].astype(F32)
            out_ref[pl.ds(r0, L), v_l[h]] = (hn * hg_ref[:, v_l[h]] * gate).astype(BF16)
        return carry

    lax.fori_loop(0, nchunk, chunk, 0)


def _mlstm(qk, v, gate, gates, head_g, bsz, seq):
    n = bsz * seq
    hp = MLSTM_HEADS
    ngrp = NH_A // hp
    return pl.pallas_call(
        _mlstm_kernel,
        grid=(bsz, ngrp),
        in_specs=[pl.BlockSpec((seq, hp * DK_A), lambda b, h: (b, h)),
                  pl.BlockSpec((seq, hp * DK_A), lambda b, h: (b, ngrp + h)),
                  pl.BlockSpec((seq, hp * DV_A), lambda b, h: (b, h)),
                  pl.BlockSpec((seq, hp * DV_A), lambda b, h: (b, h)),
                  pl.BlockSpec((seq, 2 * NH_A), lambda b, h: (b, 0)),
                  pl.BlockSpec((1, hp * DV_A), lambda b, h: (0, h))],
        out_specs=pl.BlockSpec((seq, hp * DV_A), lambda b, h: (b, h)),
        out_shape=jax.ShapeDtypeStruct((n, V_A), BF16),
        scratch_shapes=[pltpu.VMEM((hp, DK_A, DV_A + LANES), F32),
                        pltpu.VMEM((hp, 1, LANES), F32),
                        pltpu.VMEM((hp, seq, LANES), F32),
                        pltpu.VMEM((NH_A * (seq // MLSTM_CHUNK), MLSTM_CHUNK), F32)],
        compiler_params=pltpu.CompilerParams(
            dimension_semantics=("arbitrary", "arbitrary"), vmem_limit_bytes=VMEM_LIMIT),
        name="mlstm",
    )(qk, qk, v, gate, gates, head_g)


def _out_ln_kernel(h_ref, x_ref, w_ref, g_ref, b_ref, *refs):
    out_refs, wb_ref = refs[:-1], refs[-1]
    tm, d = x_ref.shape

    @pl.when(pl.program_id(0) == 0)
    def _():
        wb_ref[...] = w_ref[...].astype(BF16)

    inv_d = 1.0 / d
    for t in range(tm // ROW_SUB):
        rows = slice(t * ROW_SUB, (t + 1) * ROW_SUB)
        hs = h_ref[rows, :]
        s1 = jnp.zeros((ROW_SUB, LANES), F32)
        s2 = jnp.zeros((ROW_SUB, LANES), F32)
        for c in range(d // LN_COL_CHUNK):
            cols = slice(c * LN_COL_CHUNK, (c + 1) * LN_COL_CHUNK)
            r = ALPHA * x_ref[rows, cols] + _dot(hs, wb_ref[:, cols])
            out_refs[0][rows, cols] = r
            for l in range(LN_COL_CHUNK // LANES):
                rl = r[:, l * LANES:(l + 1) * LANES]
                s1 = s1 + rl
                s2 = s2 + rl * rl
        mu = jnp.sum(s1, axis=1, keepdims=True) * inv_d
        var = jnp.sum(s2, axis=1, keepdims=True) * inv_d - mu * mu
        rstd = lax.rsqrt(var + LN_EPS)
        for c in range(d // LN_COL_CHUNK):
            cols = slice(c * LN_COL_CHUNK, (c + 1) * LN_COL_CHUNK)
            out = (out_refs[0][rows, cols] - mu) * rstd * g_ref[:, cols] + b_ref[:, cols]
            out_refs[0][rows, cols] = out
            if len(out_refs) > 1:
                out_refs[1][rows, cols] = out.astype(BF16)


def _out_ln(h, x2, w, ln_g, ln_b, with_bf16, tm=512):
    n = x2.shape[0]
    row = pl.BlockSpec((tm, D_MODEL), lambda i: (i, 0))
    vec = pl.BlockSpec((1, D_MODEL), lambda i: (0, 0))
    out_shape = [jax.ShapeDtypeStruct((n, D_MODEL), F32)]
    out_specs = [row]
    if with_bf16:
        out_shape.append(jax.ShapeDtypeStruct((n, D_MODEL), BF16))
        out_specs.append(row)
    return pl.pallas_call(
        _out_ln_kernel,
        grid=(n // tm,),
        in_specs=[row, row,
                  pl.BlockSpec((None, D_MODEL, D_MODEL), lambda i: (0, 0, 0), pipeline_mode=pl.Buffered(1)),
                  vec, vec],
        out_specs=out_specs,
        out_shape=out_shape,
        scratch_shapes=[pltpu.VMEM((D_MODEL, D_MODEL), BF16)],
        compiler_params=pltpu.CompilerParams(
            dimension_semantics=("arbitrary",), vmem_limit_bytes=VMEM_LIMIT),
        name="out_ln",
    )(h, x2, w, ln_g, ln_b)


def _b_proj_kernel(x_ref, wk_ref, wv_ref, wq_ref, wz_ref, k_ref, v_ref, q_ref, zg_ref):
    wk = wk_ref[...].astype(BF16)
    wv = wv_ref[...].astype(BF16)
    wq = wq_ref[...].astype(BF16)
    wz = wz_ref[...].astype(BF16)
    for t in range(x_ref.shape[0] // ROW_SUB):
        rows = slice(t * ROW_SUB, (t + 1) * ROW_SUB)
        xs = x_ref[rows, :]
        k_ref[rows, :] = _dot(xs, wk).astype(BF16)
        v_ref[rows, :] = _dot(xs, wv).astype(BF16)
        q_ref[rows, :] = (_dot(xs, wq) * (DH_B ** -0.5)).astype(BF16)
        z = _dot(xs, wz)
        zg_ref[rows, :] = (z * jax.nn.sigmoid(z)).astype(BF16)


def _b_proj(x1b, kv_w, b_w_in, tm=2048, tn=256):
    n = x1b.shape[0]
    nb = W_B // tn
    wspec = lambda g: pl.BlockSpec((D_MODEL, tn), lambda i, j, g=g: (0, g * nb + j))
    wspec3 = lambda g: pl.BlockSpec((None, D_MODEL, tn), lambda i, j, g=g: (0, 0, g * nb + j))
    ospec = pl.BlockSpec((tm, tn), lambda i, j: (i, j))
    sds = jax.ShapeDtypeStruct((n, W_B), BF16)
    return pl.pallas_call(
        _b_proj_kernel,
        grid=(n // tm, nb),
        in_specs=[pl.BlockSpec((tm, D_MODEL), lambda i, j: (i, 0)), wspec(0), wspec(1), wspec3(0), wspec3(1)],
        out_specs=[ospec] * 4,
        out_shape=[sds] * 4,
        compiler_params=pltpu.CompilerParams(
            dimension_semantics=("arbitrary", "arbitrary"), vmem_limit_bytes=VMEM_LIMIT),
        name="b_proj",
    )(x1b, kv_w, kv_w, b_w_in, b_w_in)


def _sb_kernel(q_ref, k_ref, v_ref, zg_ref, o_ref, acc_ref, carry_ref):
    seq = q_ref.shape[0]
    T = SB_TILE
    HP = q_ref.shape[1] // DH_B
    rr = lax.broadcasted_iota(jnp.int32, (T, T), 0)
    cc = lax.broadcasted_iota(jnp.int32, (T, T), 1)
    below = cc < rr
    neg_suffix = jnp.where(rr > cc, -1.0, 0.0).astype(BF16)
    lanes = [slice(hh * DH_B, (hh + 1) * DH_B) for hh in range(HP)]

    def tiles(q0, k0, diagonal):
        zs = [_dot_nt(q_ref[pl.ds(q0, T), ln], k_ref[pl.ds(k0, T), ln]) for ln in lanes]
        sps, log_betas = [], []
        for z in zs:
            zb = z.astype(BF16)
            sp = jnp.maximum(zb, 0.0) + jnp.log(1.0 + jnp.exp(-jnp.abs(zb)))
            log_betas.append(zb - sp)
            sps.append(jnp.where(below, sp, 0.0) if diagonal else sp)
        betweens = [_dot(sp, neg_suffix) for sp in sps]
        alive = None
        for hh in range(HP):
            shift = betweens[hh] if diagonal else betweens[hh] + carry_ref[hh]
            a = jnp.exp(log_betas[hh] + shift.astype(BF16))
            if diagonal:
                a = jnp.where(below, a, 0.0)
            av = _dot(a, v_ref[pl.ds(k0, T), lanes[hh]])
            acc_ref[hh] = av if diagonal else acc_ref[hh] + av
            total = betweens[hh][:, 0:1] - sps[hh][:, 0:1].astype(F32)
            carry = total if diagonal else carry_ref[hh] + total
            carry_ref[hh] = carry
            alive = carry if alive is None else jnp.maximum(alive, carry)
        return (jnp.max(alive) > SB_DEAD_LOG).astype(jnp.int32)

    def qblock(qi, c):
        q0 = pl.multiple_of(qi * T, T)
        alive = tiles(q0, q0, True)

        def cond(s):
            return jnp.logical_and(s[0] >= 0, s[1] > 0)

        def body(s):
            return s[0] - 1, tiles(q0, pl.multiple_of(s[0] * T, T), False)

        lax.while_loop(cond, body, (qi - 1, alive))
        for hh in range(HP):
            o_ref[pl.ds(q0, T), lanes[hh]] = (acc_ref[hh] * zg_ref[pl.ds(q0, T), lanes[hh]].astype(F32)).astype(BF16)
        return c

    lax.fori_loop(0, seq // T, qblock, 0)


def _stickbreak(q, k, v, zg, bsz, seq):
    n = bsz * seq
    spec = pl.BlockSpec((seq, SB_HEADS * DH_B), lambda b, h: (b, h))
    return pl.pallas_call(
        _sb_kernel,
        grid=(bsz, NH_B // SB_HEADS),
        in_specs=[spec] * 4,
        out_specs=spec,
        out_shape=jax.ShapeDtypeStruct((n, W_B), BF16),
        scratch_shapes=[pltpu.VMEM((SB_HEADS, SB_TILE, DH_B), F32),
                        pltpu.VMEM((SB_HEADS, SB_TILE, 1), F32)],
        compiler_params=pltpu.CompilerParams(
            dimension_semantics=("arbitrary", "arbitrary"), vmem_limit_bytes=VMEM_LIMIT),
        name="stickbreak",
    )(q, k, v, zg)


def kernel(x, a_w_in, a_gate_b, a_conv_w, a_conv_b, a_head_g, a_w_out, a_ln_g, a_ln_b,
           kv_w, b_w_in, b_w_out, b_ln_g, b_ln_b):
    bsz, seq, d = x.shape
    assert d == D_MODEL and a_w_in.shape[0] == 1 and b_w_in.shape[0] == 1
    n = bsz * seq
    x2 = x.reshape(n, d)

    w_in_t = jnp.swapaxes(a_w_in, 1, 2)
    w_main = _w_prep(w_in_t, 2 * QK_A + 3 * V_A)
    qk, v, gate, gates = _a_in_proj(x2, w_main, w_in_t, a_gate_b[0].reshape(1, -1),
                                    a_conv_w[0], a_conv_b[0].reshape(1, -1), seq)
    hg = _mlstm(qk, v, gate, gates, a_head_g[0].reshape(1, -1), bsz, seq)
    x1, x1b = _out_ln(hg, x2, a_w_out, a_ln_g[0].reshape(1, -1), a_ln_b[0].reshape(1, -1), True)

    k_sh, v_sh, q, zg = _b_proj(x1b, kv_w, b_w_in)
    att = _stickbreak(q, k_sh, v_sh, zg, bsz, seq)
    (out,) = _out_ln(att, x1, b_w_out, b_ln_g[0].reshape(1, -1), b_ln_b[0].reshape(1, -1), False)
    return out.reshape(bsz, seq, d)
```
